```python
import jax, jax.numpy as jnp
from jax import lax
import numpy as np

D_MODEL = 1024
BATCH = 4
SEQ = 8192
DEPTH = 2
DEC_BATCH = 32
DEC_SEQ = 16
PAST_LEN = 4096

CHUNK = 64
PLE_DIM = 256
EPS = 1e-6
N_BRANCH = 3
SC_WIDTH = D_MODEL
SC_KERNEL = 3
SSM_INNER = 2 * D_MODEL
SSM_HEAD_DIM = 64
SSM_HEADS = SSM_INNER // SSM_HEAD_DIM
SSM_GROUPS = 4
SSM_STATE = 128
SSM_CONV = 4
SSM_CONV_DIM = SSM_INNER + 2 * SSM_GROUPS * SSM_STATE
CF_WIDTH = D_MODEL
CF_KERNEL = 31
FF_DIM = 2816
FF_KERNEL = 3
OFF_SC = N_BRANCH * D_MODEL
OFF_Z = OFF_SC + 3 * SC_WIDTH
OFF_XBC = OFF_Z + SSM_INNER
OFF_DT = OFF_XBC + SSM_CONV_DIM
OFF_CF = OFF_DT + SSM_HEADS
N_IN = OFF_CF + 2 * CF_WIDTH

kernel_name = "hybrid_streaming_encoder_step"


def rmsnorm(x, g):
    xf = x.astype(jnp.float32)
    xf = xf * lax.rsqrt(jnp.mean(xf * xf, axis=-1, keepdims=True) + EPS)
    return xf.astype(x.dtype) * g


def layernorm(x, g, b):
    xf = x.astype(jnp.float32)
    mu = jnp.mean(xf, axis=-1, keepdims=True)
    xc = xf - mu
    var = jnp.mean(xc * xc, axis=-1, keepdims=True)
    return (xc * lax.rsqrt(var + EPS)).astype(x.dtype) * g + b


def causal_dwconv(x, prev, w, b=None):
    k = w.shape[0]
    xp = jnp.concatenate([prev.astype(x.dtype), x], axis=1)
    y = lax.conv_general_dilated(xp, w[:, None, :].astype(x.dtype), window_strides=(1,),
                                 padding='VALID', dimension_numbers=('NWC', 'WIO', 'NWC'),
                                 feature_group_count=x.shape[-1])
    if b is not None:
        y = y + b
    return y, xp[:, xp.shape[1] - (k - 1):]


def ssd_scan(x, dt, a, bmat, cmat, h0):
    bsz, t, nh, hp = x.shape
    tp = -(-t // CHUNK) * CHUNK
    pad = tp - t

    def pad_t(z):
        return jnp.pad(z, [(0, 0), (0, pad)] + [(0, 0)] * (z.ndim - 2))

    x, dt, bmat, cmat = pad_t(x), pad_t(dt), pad_t(bmat), pad_t(cmat)
    nc = tp // CHUNK
    r = nh // SSM_GROUPS
    f32 = jnp.float32
    xc = x.reshape(bsz, nc, CHUNK, SSM_GROUPS, r, hp).astype(f32)
    dtc = dt.reshape(bsz, nc, CHUNK, SSM_GROUPS, r).astype(f32)
    bc = bmat.reshape(bsz, nc, CHUNK, SSM_GROUPS, SSM_STATE).astype(f32)
    cc = cmat.reshape(bsz, nc, CHUNK, SSM_GROUPS, SSM_STATE).astype(f32)
    acs = jnp.cumsum(dtc * a.reshape(SSM_GROUPS, r), axis=2)
    diff = acs[:, :, :, None] - acs[:, :, None, :]
    causal = jnp.tril(jnp.ones((CHUNK, CHUNK), dtype=bool))[None, None, :, :, None, None]
    lmat = jnp.exp(jnp.where(causal, diff, -jnp.inf))
    xdt = xc * dtc[..., None]
    cb = jnp.einsum('bclgn,bcsgn->bclsg', cc, bc)
    y_diag = jnp.einsum('bclsgr,bcsgrp->bclgrp', cb[..., None] * lmat, xdt)
    a_last = acs[:, :, -1]
    decay_to_end = jnp.exp(a_last[:, :, None] - acs)
    chunk_states = jnp.einsum('bcsgn,bcsgr,bcsgrp->bcgrpn', bc, decay_to_end, xdt)

    def step(h_prev, inp):
        s_c, al = inp
        h_new = jnp.exp(al)[..., None, None] * h_prev + s_c
        return h_new, h_prev

    h0g = h0.reshape(bsz, SSM_GROUPS, r, hp, SSM_STATE).astype(f32)
    h_final, h_in = lax.scan(step, h0g, (jnp.moveaxis(chunk_states, 1, 0), jnp.moveaxis(a_last, 1, 0)))
    h_in = jnp.moveaxis(h_in, 0, 1)
    y_off = jnp.einsum('bclgn,bcgrpn,bclgr->bclgrp', cc, h_in, jnp.exp(acs))
    y = (y_diag + y_off).reshape(bsz, tp, nh, hp)[:, :t]
    return y, h_final.reshape(bsz, nh, hp, SSM_STATE)


def hybrid_layer(h, p_i, st_sc, st_mconv, st_ssm, st_cf, st_ff, L):
    bsz, t, _ = h.shape
    w_in = L['w_in']
    n = rmsnorm(h, L['g_mix'])

    def proj(lo, hi):
        return n @ w_in[:, lo:hi]

    g_a, g_b, g_c = jnp.split(jax.nn.sigmoid(proj(0, OFF_SC)), N_BRANCH, axis=-1)

    s_b, s_c, s_h = jnp.split(proj(OFF_SC, OFF_Z), 3, axis=-1)
    u, new_sc = causal_dwconv(s_c * s_h, st_sc, L['sc_conv_w'])
    y_a = (s_b * u) @ L['w_sc_out']

    z = proj(OFF_Z, OFF_XBC)
    xbc, new_mconv = causal_dwconv(proj(OFF_XBC, OFF_DT), st_mconv, L['ssm_conv_w'], L['ssm_conv_b'])
    xbc = jax.nn.silu(xbc)
    gn = SSM_GROUPS * SSM_STATE
    xs = xbc[..., :SSM_INNER].reshape(bsz, t, SSM_HEADS, SSM_HEAD_DIM)
    bm = xbc[..., SSM_INNER:SSM_INNER + gn].reshape(bsz, t, SSM_GROUPS, SSM_STATE)
    cm = xbc[..., SSM_INNER + gn:].reshape(bsz, t, SSM_GROUPS, SSM_STATE)
    dt = jax.nn.softplus(proj(OFF_DT, OFF_CF).astype(jnp.float32) + L['ssm_dt_bias'].astype(jnp.float32))
    a = -jnp.exp(L['ssm_a_log'].astype(jnp.float32))
    ys, new_ssm = ssd_scan(xs, dt, a, bm, cm, st_ssm)
    ys = ys + L['ssm_d'].astype(jnp.float32)[:, None] * xs.astype(jnp.float32)
    ys = ys.reshape(bsz, t, SSM_INNER) * jax.nn.silu(z.astype(jnp.float32))
    yg = ys.reshape(bsz, t, SSM_GROUPS, SSM_INNER // SSM_GROUPS)
    yg = yg * lax.rsqrt(jnp.mean(yg * yg, axis=-1, keepdims=True) + EPS)
    ys = yg.reshape(bsz, t, SSM_INNER).astype(h.dtype) * L['ssm_norm_g']
    y_b = ys @ L['w_ssm_out']

    c_a, c_g = jnp.split(proj(OFF_CF, N_IN), 2, axis=-1)
    v, new_cf = causal_dwconv(c_a * jax.nn.sigmoid(c_g), st_cf, L['cf_conv_w'], L['cf_conv_b'])
    y_c = jax.nn.silu(layernorm(v, L['cf_ln_g'], L['cf_ln_b'])) @ L['w_cf_out']

    h = h + (g_a * y_a + g_b * y_b + g_c * y_c) @ L['w_o']

    up, new_ff = causal_dwconv(rmsnorm(h, L['g_ffn']) @ L['w_up'], st_ff, L['ff_conv_w'], L['ff_conv_b'])
    u_a, u_b = jnp.split(up, 2, axis=-1)
    h = h + (jax.nn.silu(u_a) * u_b) @ L['w_down']

    h = h + (p_i @ L['w_ple_proj']) * jax.nn.sigmoid(rmsnorm(h, L['g_ple']) @ L['w_ple_gate'])
    return h, (new_sc, new_mconv, new_ssm, new_cf, new_ff)


def run_trunk(x, p, st_sc, st_mconv, st_ssm, st_cf, st_ff, W, g_final):
    h = x
    outs = [[], [], [], [], []]
    for i in range(DEPTH):
        L = {k: v[i] for k, v in W.items()}
        h, new = hybrid_layer(h, p[i], st_sc[i], st_mconv[i], st_ssm[i], st_cf[i], st_ff[i], L)
        for lst, s in zip(outs, new):
            lst.append(s)
    y = rmsnorm(h, g_final)
    return y, [jnp.stack(lst) for lst in outs]


def setup_inputs(seed: int = 0) -> dict:
    key = jax.random.key(seed)
    ks = iter(jax.random.split(key, 48))
    f32 = jnp.float32

    def nrm(shape, scale):
        return jax.random.normal(next(ks), shape, f32) * scale

    def gain(shape):
        return 1.0 + nrm(shape, 0.05)

    dt0 = jnp.exp(jax.random.uniform(next(ks), (DEPTH, SSM_HEADS), f32, np.log(1e-3), np.log(1e-1)))
    dt_bias = dt0 + jnp.log(-jnp.expm1(-dt0))
    a_log = jnp.log(jax.random.uniform(next(ks), (DEPTH, SSM_HEADS), f32, 1.0, 16.0))
    return {
        'x_prompt': nrm((BATCH, SEQ, D_MODEL), 1.0),
        'x_sample': nrm((DEC_BATCH, DEC_SEQ, D_MODEL), 1.0),
        'p_prompt': nrm((DEPTH, BATCH, SEQ, PLE_DIM), 1.0),
        'p_sample': nrm((DEPTH, DEC_BATCH, DEC_SEQ, PLE_DIM), 1.0),
        'state_short_conv': nrm((DEPTH, DEC_BATCH, SC_KERNEL - 1, SC_WIDTH), 0.5),
        'state_ssm_conv': nrm((DEPTH, DEC_BATCH, SSM_CONV - 1, SSM_CONV_DIM), 1.0),
        'state_ssm': nrm((DEPTH, DEC_BATCH, SSM_HEADS, SSM_HEAD_DIM, SSM_STATE), 0.1),
        'state_cf_conv': nrm((DEPTH, DEC_BATCH, CF_KERNEL - 1, CF_WIDTH), 0.5),
        'state_ffn_conv': nrm((DEPTH, DEC_BATCH, FF_KERNEL - 1, 2 * FF_DIM), 1.0),
        'g_mix': gain((DEPTH, D_MODEL)),
        'w_in': nrm((DEPTH, D_MODEL, N_IN), D_MODEL ** -0.5),
        'sc_conv_w': nrm((DEPTH, SC_KERNEL, SC_WIDTH), SC_KERNEL ** -0.5),
        'w_sc_out': nrm((DEPTH, SC_WIDTH, D_MODEL), SC_WIDTH ** -0.5),
        'ssm_conv_w': nrm((DEPTH, SSM_CONV, SSM_CONV_DIM), SSM_CONV ** -0.5),
        'ssm_conv_b': nrm((DEPTH, SSM_CONV_DIM), 0.02),
        'ssm_dt_bias': dt_bias,
        'ssm_a_log': a_log,
        'ssm_d': gain((DEPTH, SSM_HEADS)),
        'ssm_norm_g': gain((DEPTH, SSM_INNER)),
        'w_ssm_out': nrm((DEPTH, SSM_INNER, D_MODEL), SSM_INNER ** -0.5),
        'cf_conv_w': nrm((DEPTH, CF_KERNEL, CF_WIDTH), CF_KERNEL ** -0.5),
        'cf_conv_b': nrm((DEPTH, CF_WIDTH), 0.02),
        'cf_ln_g': gain((DEPTH, CF_WIDTH)),
        'cf_ln_b': nrm((DEPTH, CF_WIDTH), 0.02),
        'w_cf_out': nrm((DEPTH, CF_WIDTH, D_MODEL), CF_WIDTH ** -0.5),
        'w_o': nrm((DEPTH, D_MODEL, D_MODEL), D_MODEL ** -0.5),
        'g_ffn': gain((DEPTH, D_MODEL)),
        'w_up': nrm((DEPTH, D_MODEL, 2 * FF_DIM), D_MODEL ** -0.5),
        'ff_conv_w': nrm((DEPTH, FF_KERNEL, 2 * FF_DIM), FF_KERNEL ** -0.5),
        'ff_conv_b': nrm((DEPTH, 2 * FF_DIM), 0.02),
        'w_down': nrm((DEPTH, FF_DIM, D_MODEL), FF_DIM ** -0.5),
        'g_ple': gain((DEPTH, D_MODEL)),
        'w_ple_gate': nrm((DEPTH, D_MODEL, D_MODEL), D_MODEL ** -0.5),
        'w_ple_proj': nrm((DEPTH, PLE_DIM, D_MODEL), PLE_DIM ** -0.5),
        'g_final': gain((D_MODEL,)),
    }


def reference(x_prompt, x_sample, p_prompt, p_sample, state_short_conv, state_ssm_conv, state_ssm,
              state_cf_conv, state_ffn_conv, g_mix, w_in, sc_conv_w, w_sc_out, ssm_conv_w, ssm_conv_b,
              ssm_dt_bias, ssm_a_log, ssm_d, ssm_norm_g, w_ssm_out, cf_conv_w, cf_conv_b, cf_ln_g,
              cf_ln_b, w_cf_out, w_o, g_ffn, w_up, ff_conv_w, ff_conv_b, w_down, g_ple, w_ple_gate,
              w_ple_proj, g_final):
    W = {
        'g_mix': g_mix, 'w_in': w_in, 'sc_conv_w': sc_conv_w, 'w_sc_out': w_sc_out,
        'ssm_conv_w': ssm_conv_w, 'ssm_conv_b': ssm_conv_b, 'ssm_dt_bias': ssm_dt_bias,
        'ssm_a_log': ssm_a_log, 'ssm_d': ssm_d, 'ssm_norm_g': ssm_norm_g, 'w_ssm_out': w_ssm_out,
        'cf_conv_w': cf_conv_w, 'cf_conv_b': cf_conv_b, 'cf_ln_g': cf_ln_g, 'cf_ln_b': cf_ln_b,
        'w_cf_out': w_cf_out, 'w_o': w_o, 'g_ffn': g_ffn, 'w_up': w_up, 'ff_conv_w': ff_conv_w,
        'ff_conv_b': ff_conv_b, 'w_down': w_down, 'g_ple': g_ple, 'w_ple_gate': w_ple_gate,
        'w_ple_proj': w_ple_proj,
    }
    bp, dtp = x_prompt.shape[0], x_prompt.dtype
    z_sc = jnp.zeros((DEPTH, bp, SC_KERNEL - 1, SC_WIDTH), dtp)
    z_mconv = jnp.zeros((DEPTH, bp, SSM_CONV - 1, SSM_CONV_DIM), dtp)
    z_ssm = jnp.zeros((DEPTH, bp, SSM_HEADS, SSM_HEAD_DIM, SSM_STATE), jnp.float32)
    z_cf = jnp.zeros((DEPTH, bp, CF_KERNEL - 1, CF_WIDTH), dtp)
    z_ff = jnp.zeros((DEPTH, bp, FF_KERNEL - 1, 2 * FF_DIM), dtp)
    y_prompt, sp = run_trunk(x_prompt, p_prompt, z_sc, z_mconv, z_ssm, z_cf, z_ff, W, g_final)
    y_sample, ss = run_trunk(x_sample, p_sample, state_short_conv, state_ssm_conv, state_ssm,
                             state_cf_conv, state_ffn_conv, W, g_final)
    return (y_prompt, y_sample, sp[0], sp[1], sp[2], sp[3], sp[4], ss[0], ss[1], ss[2], ss[3], ss[4])
```

```python
import functools

import jax
import jax.numpy as jnp
from jax import lax
from jax.experimental import pallas as pl
from jax.experimental.pallas import tpu as pltpu

F32 = jnp.float32
BF16 = jnp.bfloat16

D_MODEL = 1024
DEPTH = 2
PLE_DIM = 256
EPS = 1e-6
SC_KERNEL = 3
SSM_INNER = 2 * D_MODEL
SSM_HEAD_DIM = 64
SSM_HEADS = SSM_INNER // SSM_HEAD_DIM
SSM_GROUPS = 4
SSM_STATE = 128
SSM_CONV = 4
SSM_GN = SSM_GROUPS * SSM_STATE
SSM_CONV_DIM = SSM_INNER + 2 * SSM_GN
CF_KERNEL = 31
FF_DIM = 2816
FF_KERNEL = 3
OFF_SC = 3 * D_MODEL
OFF_Z = OFF_SC + 3 * D_MODEL
OFF_XBC = OFF_Z + SSM_INNER
OFF_DT = OFF_XBC + SSM_CONV_DIM
OFF_CF = OFF_DT + SSM_HEADS
N_IN = OFF_CF + 2 * D_MODEL

LANES = 128
SUBLANES = 8
SSD_CHUNK = 128
GROUP_W = SSM_INNER // SSM_GROUPS
VMEM_LIMIT = 56 * 1024 * 1024


def _dot(a, b):
    return jnp.dot(a, b, preferred_element_type=F32)


def _sigmoid(x):
    return 1.0 / (1.0 + jnp.exp(-x))


def _silu(x):
    return x * _sigmoid(x)


def _softplus(x):
    return jnp.maximum(x, 0.0) + jnp.log1p(jnp.exp(-jnp.abs(x)))


def _rms(x, g):
    return x * lax.rsqrt(jnp.mean(x * x, axis=-1, keepdims=True) + EPS) * g


def _split_bf16(x, terms):
    out = []
    r = x
    for _ in range(terms):
        p = r.astype(BF16)
        out.append(p)
        r = r - p.astype(F32)
    return out


def _pad_rows(k):
    return -(-(k - 1) // SUBLANES) * SUBLANES


def _chunk_pos(i, cpb, rc):
    if cpb == 1:
        return i, 0
    b = i // cpb
    return b, pl.multiple_of((i - b * cpb) * rc, rc)


def _lane_blocks(c):
    return [slice(lb * LANES, (lb + 1) * LANES) for lb in range(c // LANES)]


def _load_conv_tail(xpad, pad, k, st_ref):
    for lb, ls in enumerate(_lane_blocks(st_ref.shape[-1])):
        xpad[:, lb, pad - (k - 1):pad, :] = st_ref[:, :, ls]


def _carry_conv_tail(xpad, pad, k, tt, new_ref, is_last):
    tail = xpad[:, :, pad + tt - (k - 1):pad + tt, :]

    @pl.when(is_last)
    def _():
        for lb, ls in enumerate(_lane_blocks(new_ref.shape[-1])):
            new_ref[:, :, ls] = tail[:, lb]

    xpad[:, :, pad - (k - 1):pad, :] = tail


def _store_tile(xpad, pad, val, nb, tt):
    for lb, ls in enumerate(_lane_blocks(val.shape[-1])):
        xpad[:, lb, pad:pad + tt, :] = val[:, ls].reshape(nb, tt, LANES)


def _ac_kernel(h_ref, stsc_ref, stcf_ref, gmix_ref, wg_ref, wsc_ref, wcf_ref, scw_ref, wsco_ref,
               cfw_ref, cfb_ref, lng_ref, lnb_ref, wcfo_ref,
               n_ref, mac_ref, gb_ref, nsc_ref, ncf_ref,
               xsc, xcf, proj, lhs, ya, yc, *, nb, tt, rc):
    t = pl.program_id(1)
    is_last = t == pl.num_programs(1) - 1
    rows = nb * tt
    cpb = tt // rc
    psc, pcf = _pad_rows(SC_KERNEL), _pad_rows(CF_KERNEL)

    @pl.when(t == 0)
    def _():
        _load_conv_tail(xsc, psc, SC_KERNEL, stsc_ref)
        _load_conv_tail(xcf, pcf, CF_KERNEL, stcf_ref)

    h = h_ref[...].reshape(rows, D_MODEL)
    n = _rms(h, gmix_ref[...]).astype(BF16)
    n_ref[...] = n.reshape(nb, tt, D_MODEL)

    proj[...] = _dot(n, wsc_ref[...])

    def a_fill(i, c):
        b, r = _chunk_pos(i, cpb, rc)
        r2 = pl.multiple_of(i * rc, rc)
        for lb, ls in enumerate(_lane_blocks(D_MODEL)):
            s_c = proj[pl.ds(r2, rc), D_MODEL + lb * LANES:D_MODEL + (lb + 1) * LANES]
            s_h = proj[pl.ds(r2, rc), 2 * D_MODEL + lb * LANES:2 * D_MODEL + (lb + 1) * LANES]
            xsc[b, lb, pl.ds(psc + r, rc), :] = s_c * s_h
        return c

    lax.fori_loop(0, rows // rc, a_fill, 0)

    def a_conv(i, c):
        b, r = _chunk_pos(i, cpb, rc)
        r2 = pl.multiple_of(i * rc, rc)
        for lb, ls in enumerate(_lane_blocks(D_MODEL)):
            u = None
            for k in range(SC_KERNEL):
                term = xsc[b, lb, pl.ds(psc - (SC_KERNEL - 1) + k + r, rc), :] * scw_ref[k:k + 1, ls]
                u = term if u is None else u + term
            lhs[pl.ds(r2, rc), ls] = (proj[pl.ds(r2, rc), ls] * u).astype(BF16)
        return c

    lax.fori_loop(0, rows // rc, a_conv, 0)
    _carry_conv_tail(xsc, psc, SC_KERNEL, tt, nsc_ref, is_last)
    ya[...] = _dot(lhs[...], wsco_ref[...])

    proj[:, 0:2 * D_MODEL] = _dot(n, wcf_ref[...])

    def c_fill(i, c):
        b, r = _chunk_pos(i, cpb, rc)
        r2 = pl.multiple_of(i * rc, rc)
        for lb, ls in enumerate(_lane_blocks(D_MODEL)):
            c_g = proj[pl.ds(r2, rc), D_MODEL + lb * LANES:D_MODEL + (lb + 1) * LANES]
            xcf[b, lb, pl.ds(pcf + r, rc), :] = proj[pl.ds(r2, rc), ls] * _sigmoid(c_g)
        return c

    lax.fori_loop(0, rows // rc, c_fill, 0)

    def c_conv(i, c):
        b, r = _chunk_pos(i, cpb, rc)
        r2 = pl.multiple_of(i * rc, rc)
        cols = []
        for lb, ls in enumerate(_lane_blocks(D_MODEL)):
            acc = None
            for k in range(CF_KERNEL):
                term = xcf[b, lb, pl.ds(pcf - (CF_KERNEL - 1) + k + r, rc), :] * cfw_ref[k:k + 1, ls]
                acc = term if acc is None else acc + term
            cols.append(acc)
        v = jnp.concatenate(cols, axis=-1) + cfb_ref[...]
        mu = jnp.mean(v, axis=-1, keepdims=True)
        vc = v - mu
        var = jnp.mean(vc * vc, axis=-1, keepdims=True)
        ln = vc * lax.rsqrt(var + EPS) * lng_ref[...] + lnb_ref[...]
        lhs[pl.ds(r2, rc), :] = _silu(ln).astype(BF16)
        return c

    lax.fori_loop(0, rows // rc, c_conv, 0)
    _carry_conv_tail(xcf, pcf, CF_KERNEL, tt, ncf_ref, is_last)
    yc[...] = _dot(lhs[...], wcfo_ref[...])

    proj[...] = _dot(n, wg_ref[...])

    def g_merge(i, c):
        b, r = _chunk_pos(i, cpb, rc)
        r2 = pl.multiple_of(i * rc, rc)
        g_a = _sigmoid(proj[pl.ds(r2, rc), 0:D_MODEL])
        g_b = _sigmoid(proj[pl.ds(r2, rc), D_MODEL:2 * D_MODEL])
        g_c = _sigmoid(proj[pl.ds(r2, rc), 2 * D_MODEL:3 * D_MODEL])
        mac_ref[b, pl.ds(r, rc), :] = g_a * ya[pl.ds(r2, rc), :] + g_c * yc[pl.ds(r2, rc), :]
        gb_ref[b, pl.ds(r, rc), :] = g_b
        return c

    lax.fori_loop(0, rows // rc, g_merge, 0)


def _ssm_kernel(n_ref, h_ref, mac_ref, gb_ref, stmc_ref, stssm_ref,
                wz_ref, wxbc_ref, wdt_ref, cw_ref, cb_ref, dtb_ref, alog_ref, de_ref, ng_ref,
                wso_ref, wo_ref, e_ref,
                h1_ref, nmc_ref, nssm_ref,
                xpad, zs, xs_s, bs_s, cs_s, dt_s, acs_s, acst_s, ht, ys, *, nb, tt, rc):
    t = pl.program_id(1)
    is_last = t == pl.num_programs(1) - 1
    rows = nb * tt
    cpb = tt // rc
    pmc = _pad_rows(SSM_CONV)
    L = SSD_CHUNK
    seq_mode = tt >= L
    lv = L if seq_mode else tt
    n_chunks = rows // lv

    @pl.when(t == 0)
    def _():
        _load_conv_tail(xpad, pmc, SSM_CONV, stmc_ref)

    def load_state(b):
        for j in range(SSM_INNER // LANES):
            ht[:, j * LANES:(j + 1) * LANES] = stssm_ref[b, j * LANES:(j + 1) * LANES, :].T

    def store_state(b):
        for j in range(SSM_INNER // LANES):
            nssm_ref[b, j * LANES:(j + 1) * LANES, :] = ht[:, j * LANES:(j + 1) * LANES].T

    if seq_mode:
        @pl.when(t == 0)
        def _():
            load_state(0)
    else:
        xs_s[...] = jnp.zeros_like(xs_s)
        bs_s[...] = jnp.zeros_like(bs_s)
        cs_s[...] = jnp.zeros_like(cs_s)
        dt_s[...] = jnp.zeros_like(dt_s)

    n = n_ref[...].reshape(rows, D_MODEL)
    zs[...] = _dot(n, wz_ref[...])
    _store_tile(xpad, pmc, _dot(n, wxbc_ref[...]), nb, tt)
    dt_all = _softplus(_dot(n, wdt_ref[...]) + dtb_ref[...])
    if seq_mode:
        dt_s[...] = dt_all
    else:
        for b in range(nb):
            dt_s[b * L:b * L + tt, :] = dt_all[b * tt:(b + 1) * tt, :]

    def conv(i, c):
        b, r = _chunk_pos(i, cpb, rc)
        dst = pl.multiple_of(i * (rc if seq_mode else L), rc)
        for lb, ls in enumerate(_lane_blocks(SSM_CONV_DIM)):
            acc = None
            for k in range(SSM_CONV):
                term = xpad[b, lb, pl.ds(pmc - (SSM_CONV - 1) + k + r, rc), :] * cw_ref[k:k + 1, ls]
                acc = term if acc is None else acc + term
            xbc = _silu(acc + cb_ref[:, ls])
            if ls.start < SSM_INNER:
                xs_s[pl.ds(dst, rc), ls] = xbc
            elif ls.start < SSM_INNER + SSM_GN:
                bs_s[pl.ds(dst, rc), ls.start - SSM_INNER:ls.stop - SSM_INNER] = xbc
            else:
                cs_s[pl.ds(dst, rc), ls.start - SSM_INNER - SSM_GN:ls.stop - SSM_INNER - SSM_GN] = xbc
        return c

    lax.fori_loop(0, rows // rc, conv, 0)
    _carry_conv_tail(xpad, pmc, SSM_CONV, tt, nmc_ref, is_last)

    a_row = -jnp.exp(alog_ref[...])
    ii = lax.broadcasted_iota(jnp.int32, (L, L), 0)
    jj = lax.broadcasted_iota(jnp.int32, (L, L), 1)
    causal = ii >= jj
    tril = jnp.where(causal, 1.0, 0.0).astype(BF16)
    low_half = lax.broadcasted_iota(jnp.int32, (L, LANES), 1) < SSM_HEAD_DIM

    def expand(x, terms=2):
        parts = _split_bf16(x, terms)
        out = _dot(parts[0], e_ref[...])
        for p in parts[1:]:
            out = out + _dot(p, e_ref[...])
        return out

    def chunk(c, carry):
        if not seq_mode:
            load_state(c)
        r0 = pl.multiple_of(c * L, L)
        rv = pl.multiple_of(c * lv, lv)
        dtc = dt_s[pl.ds(r0, L), :]
        parts = _split_bf16(dtc * a_row, 3)
        acs = _dot(tril, parts[0]) + _dot(tril, parts[1]) + _dot(tril, parts[2])
        acs_s[...] = acs
        acst_s[...] = acs.T
        a_last = acs[L - 1:L, :]
        dt_e = expand(dtc)
        eacs_e = expand(jnp.exp(acs))
        dec_e = expand(jnp.exp(a_last - acs))
        elast_e = expand(jnp.broadcast_to(jnp.exp(a_last), (SUBLANES, LANES)))[0:1, :]
        for g in range(SSM_GROUPS):
            gs = slice(g * GROUP_W, (g + 1) * GROUP_W)
            ns = slice(g * SSM_STATE, (g + 1) * SSM_STATE)
            xs_g = xs_s[pl.ds(r0, L), gs]
            xdt = xs_g * dt_e[:, gs]
            xdt_b = xdt.astype(BF16)
            xdw_b = (xdt * dec_e[:, gs]).astype(BF16)
            c_g = cs_s[pl.ds(r0, L), ns].astype(BF16)
            bt_g = bs_s[pl.ds(r0, L), ns].T.astype(BF16)
            cb = _dot(c_g, bt_g)
            ht_g = ht[:, gs]
            y_off = _dot(c_g, ht_g.astype(BF16)) * eacs_e[:, gs]
            ht[:, gs] = ht_g * elast_e[:, gs] + _dot(bt_g, xdw_b)
            y_cols = []
            for pr in range(GROUP_W // LANES):
                hd = g * (GROUP_W // SSM_HEAD_DIM) + 2 * pr
                xp = xdt_b[:, pr * LANES:(pr + 1) * LANES]
                res = []
                for hh in (hd, hd + 1):
                    diff = acs_s[:, hh:hh + 1] - acst_s[hh:hh + 1, :]
                    m = (cb * jnp.exp(jnp.where(causal, diff, -jnp.inf))).astype(BF16)
                    res.append(_dot(m, xp))
                y_cols.append(jnp.where(low_half, res[0], res[1]))
            y = jnp.concatenate(y_cols, axis=-1) + y_off + de_ref[:, gs] * xs_g
            y = y[0:lv, :] * _silu(zs[pl.ds(rv, lv), gs])
            y = y * lax.rsqrt(jnp.mean(y * y, axis=-1, keepdims=True) + EPS)
            ys[pl.ds(rv, lv), gs] = (y * ng_ref[:, gs]).astype(BF16)
        if not seq_mode:
            store_state(c)
        return carry

    lax.fori_loop(0, n_chunks, chunk, 0)

    if seq_mode:
        @pl.when(is_last)
        def _():
            store_state(0)

    y_b = _dot(ys[...], wso_ref[...])
    merged = mac_ref[...].reshape(rows, D_MODEL) + gb_ref[...].reshape(rows, D_MODEL) * y_b
    h1 = h_ref[...].reshape(rows, D_MODEL) + _dot(merged.astype(BF16), wo_ref[...])
    h1_ref[...] = h1.reshape(nb, tt, D_MODEL)


def _ffn_kernel(h_ref, p_ref, stff_ref, gffn_ref, wup_ref, fw_ref, fb_ref, wdn_ref, gple_ref,
                wpg_ref, wpp_ref, gfin_ref,
                out_ref, nff_ref,
                xpad, lhs, *, nb, tt, rc, final):
    t = pl.program_id(1)
    is_last = t == pl.num_programs(1) - 1
    rows = nb * tt
    cpb = tt // rc
    pff = _pad_rows(FF_KERNEL)

    @pl.when(t == 0)
    def _():
        _load_conv_tail(xpad, pff, FF_KERNEL, stff_ref)

    h1 = h_ref[...].reshape(rows, D_MODEL)
    nf = _rms(h1, gffn_ref[...]).astype(BF16)
    _store_tile(xpad, pff, _dot(nf, wup_ref[...]), nb, tt)

    def conv(i, c):
        b, r = _chunk_pos(i, cpb, rc)
        r2 = pl.multiple_of(i * rc, rc)
        for lb, ls in enumerate(_lane_blocks(FF_DIM)):
            halves = []
            for blk in (lb, FF_DIM // LANES + lb):
                cs = slice(blk * LANES, (blk + 1) * LANES)
                acc = None
                for k in range(FF_KERNEL):
                    term = xpad[b, blk, pl.ds(pff - (FF_KERNEL - 1) + k + r, rc), :] * fw_ref[k:k + 1, cs]
                    acc = term if acc is None else acc + term
                halves.append(acc + fb_ref[:, cs])
            lhs[pl.ds(r2, rc), ls] = (_silu(halves[0]) * halves[1]).astype(BF16)
        return c

    lax.fori_loop(0, rows // rc, conv, 0)
    _carry_conv_tail(xpad, pff, FF_KERNEL, tt, nff_ref, is_last)

    h2 = h1 + _dot(lhs[...], wdn_ref[...])
    gate = _sigmoid(_dot(_rms(h2, gple_ref[...]).astype(BF16), wpg_ref[...]))
    pp = _dot(p_ref[...].reshape(rows, PLE_DIM).astype(BF16), wpp_ref[...])
    h3 = h2 + pp * gate
    if final:
        h3 = _rms(h3, gfin_ref[...])
    out_ref[...] = h3.reshape(nb, tt, D_MODEL)


def _tile_spec(nb, tt, c):
    return pl.BlockSpec((nb, tt, c), lambda b, t: (b, t, 0))


def _state_spec(nb, r, c):
    return pl.BlockSpec((nb, r, c), lambda b, t: (b, 0, 0))


def _const_spec(arr):
    return pl.BlockSpec(arr.shape, lambda b, t: (0,) * arr.ndim, pipeline_mode=pl.Buffered(1))


def _params():
    return pltpu.CompilerParams(dimension_semantics=("arbitrary", "arbitrary"),
                                vmem_limit_bytes=VMEM_LIMIT)


def _conv_buffer(nb, k, tt, c):
    return pltpu.VMEM((nb, c // LANES, _pad_rows(k) + tt, LANES), F32)


def _ac_call(h, st_sc, st_cf, w, nb, tt):
    bsz, tlen, _ = h.shape
    rc = min(tt, 32)
    rows = nb * tt
    consts = [w['g_mix'], w['w_gate'], w['w_sc'], w['w_cf'], w['sc_conv_w'], w['w_sc_out'],
              w['cf_conv_w'], w['cf_conv_b'], w['cf_ln_g'], w['cf_ln_b'], w['w_cf_out']]
    return pl.pallas_call(
        functools.partial(_ac_kernel, nb=nb, tt=tt, rc=rc),
        grid=(bsz // nb, tlen // tt),
        in_specs=[_tile_spec(nb, tt, D_MODEL), _state_spec(nb, SC_KERNEL - 1, D_MODEL),
                  _state_spec(nb, CF_KERNEL - 1, D_MODEL)] + [_const_spec(a) for a in consts],
        out_specs=[_tile_spec(nb, tt, D_MODEL), _tile_spec(nb, tt, D_MODEL), _tile_spec(nb, tt, D_MODEL),
                   _state_spec(nb, SC_KERNEL - 1, D_MODEL), _state_spec(nb, CF_KERNEL - 1, D_MODEL)],
        out_shape=[jax.ShapeDtypeStruct((bsz, tlen, D_MODEL), BF16),
                   jax.ShapeDtypeStruct((bsz, tlen, D_MODEL), F32),
                   jax.ShapeDtypeStruct((bsz, tlen, D_MODEL), F32),
                   jax.ShapeDtypeStruct((bsz, SC_KERNEL - 1, D_MODEL), F32),
                   jax.ShapeDtypeStruct((bsz, CF_KERNEL - 1, D_MODEL), F32)],
        scratch_shapes=[_conv_buffer(nb, SC_KERNEL, tt, D_MODEL),
                        _conv_buffer(nb, CF_KERNEL, tt, D_MODEL),
                        pltpu.VMEM((rows, 3 * D_MODEL), F32),
                        pltpu.VMEM((rows, D_MODEL), BF16),
                        pltpu.VMEM((rows, D_MODEL), F32),
                        pltpu.VMEM((rows, D_MODEL), F32)],
        compiler_params=_params(),
        name="mix_ac",
    )(h, st_sc, st_cf, *consts)


def _ssm_call(n, h, mac, gb, st_mc, st_ssm, w, nb, tt):
    bsz, tlen, _ = h.shape
    rc = min(tt, 32)
    rows = nb * tt
    L = SSD_CHUNK
    prow = rows if tt >= L else nb * L
    consts = [w['w_z'], w['w_xbc'], w['w_dt'], w['ssm_conv_w'], w['ssm_conv_b'], w['dt_bias'], w['a_log'],
              w['d_exp'], w['ssm_norm_g'], w['w_ssm_out'], w['w_o'], w['head_expand']]
    st_ssm2 = st_ssm.reshape(bsz, SSM_INNER, SSM_STATE)
    h1, nmc, nssm = pl.pallas_call(
        functools.partial(_ssm_kernel, nb=nb, tt=tt, rc=rc),
        grid=(bsz // nb, tlen // tt),
        in_specs=[_tile_spec(nb, tt, D_MODEL)] * 4
        + [_state_spec(nb, SSM_CONV - 1, SSM_CONV_DIM), _state_spec(nb, SSM_INNER, SSM_STATE)]
        + [_const_spec(a) for a in consts],
        out_specs=[_tile_spec(nb, tt, D_MODEL), _state_spec(nb, SSM_CONV - 1, SSM_CONV_DIM),
                   _state_spec(nb, SSM_INNER, SSM_STATE)],
        out_shape=[jax.ShapeDtypeStruct((bsz, tlen, D_MODEL), F32),
                   jax.ShapeDtypeStruct((bsz, SSM_CONV - 1, SSM_CONV_DIM), F32),
                   jax.ShapeDtypeStruct((bsz, SSM_INNER, SSM_STATE), F32)],
        scratch_shapes=[_conv_buffer(nb, SSM_CONV, tt, SSM_CONV_DIM),
                        pltpu.VMEM((rows, SSM_INNER), F32),
                        pltpu.VMEM((prow, SSM_INNER), F32),
                        pltpu.VMEM((prow, SSM_GN), F32),
                        pltpu.VMEM((prow, SSM_GN), F32),
                        pltpu.VMEM((prow, LANES), F32),
                        pltpu.VMEM((L, LANES), F32),
                        pltpu.VMEM((LANES, L), F32),
                        pltpu.VMEM((SSM_STATE, SSM_INNER), F32),
                        pltpu.VMEM((rows, SSM_INNER), BF16)],
        compiler_params=_params(),
        name="mix_ssm",
    )(n, h, mac, gb, st_mc, st_ssm2, *consts)
    return h1, nmc, nssm.reshape(bsz, SSM_HEADS, SSM_HEAD_DIM, SSM_STATE)


def _ffn_call(h1, p, st_ff, w, g_final, nb, tt, final):
    bsz, tlen, _ = h1.shape
    rc = min(tt, 32)
    rows = nb * tt
    consts = [w['g_ffn'], w['w_up'], w['ff_conv_w'], w['ff_conv_b'], w['w_down'], w['g_ple'],
              w['w_ple_gate'], w['w_ple_proj'], g_final]
    return pl.pallas_call(
        functools.partial(_ffn_kernel, nb=nb, tt=tt, rc=rc, final=final),
        grid=(bsz // nb, tlen // tt),
        in_specs=[_tile_spec(nb, tt, D_MODEL), _tile_spec(nb, tt, PLE_DIM),
                  _state_spec(nb, FF_KERNEL - 1, 2 * FF_DIM)] + [_const_spec(a) for a in consts],
        out_specs=[_tile_spec(nb, tt, D_MODEL), _state_spec(nb, FF_KERNEL - 1, 2 * FF_DIM)],
        out_shape=[jax.ShapeDtypeStruct((bsz, tlen, D_MODEL), F32),
                   jax.ShapeDtypeStruct((bsz, FF_KERNEL - 1, 2 * FF_DIM), F32)],
        scratch_shapes=[_conv_buffer(nb, FF_KERNEL, tt, 2 * FF_DIM),
                        pltpu.VMEM((rows, FF_DIM), BF16)],
        compiler_params=_params(),
        name="ffn_ple",
    )(h1, p, st_ff, *consts)


def _prep_layer(i, g_mix, w_in, sc_conv_w, w_sc_out, ssm_conv_w, ssm_conv_b, ssm_dt_bias, ssm_a_log, ssm_d,
                ssm_norm_g, w_ssm_out, cf_conv_w, cf_conv_b, cf_ln_g, cf_ln_b, w_cf_out, w_o, g_ffn, w_up,
                ff_conv_w, ff_conv_b, w_down, g_ple, w_ple_gate, w_ple_proj):
    wi = w_in[i]
    row = lambda v: v[i].reshape(1, -1)
    lane_pad = lambda v: jnp.pad(v, [(0, 0)] * (v.ndim - 1) + [(0, LANES - v.shape[-1])])
    head_of_channel = jnp.arange(SSM_INNER) // SSM_HEAD_DIM
    return {
        'g_mix': row(g_mix),
        'w_gate': wi[:, 0:OFF_SC].astype(BF16),
        'w_sc': wi[:, OFF_SC:OFF_Z].astype(BF16),
        'w_z': wi[:, OFF_Z:OFF_XBC].astype(BF16),
        'w_xbc': wi[:, OFF_XBC:OFF_DT].astype(BF16),
        'w_dt': lane_pad(wi[:, OFF_DT:OFF_CF]).astype(BF16),
        'w_cf': wi[:, OFF_CF:N_IN].astype(BF16),
        'sc_conv_w': sc_conv_w[i],
        'w_sc_out': w_sc_out[i].astype(BF16),
        'ssm_conv_w': ssm_conv_w[i],
        'ssm_conv_b': row(ssm_conv_b),
        'dt_bias': lane_pad(row(ssm_dt_bias)),
        'a_log': lane_pad(row(ssm_a_log)),
        'd_exp': jnp.repeat(ssm_d[i], SSM_HEAD_DIM).reshape(1, SSM_INNER),
        'ssm_norm_g': row(ssm_norm_g),
        'w_ssm_out': w_ssm_out[i].astype(BF16),
        'cf_conv_w': cf_conv_w[i],
        'cf_conv_b': row(cf_conv_b),
        'cf_ln_g': row(cf_ln_g),
        'cf_ln_b': row(cf_ln_b),
        'w_cf_out': w_cf_out[i].astype(BF16),
        'w_o': w_o[i].astype(BF16),
        'g_ffn': row(g_ffn),
        'w_up': w_up[i].astype(BF16),
        'ff_conv_w': ff_conv_w[i],
        'ff_conv_b': row(ff_conv_b),
        'w_down': w_down[i].astype(BF16),
        'g_ple': row(g_ple),
        'w_ple_gate': w_ple_gate[i].astype(BF16),
        'w_ple_proj': w_ple_proj[i].astype(BF16),
        'head_expand': (jnp.arange(LANES)[:, None] == head_of_channel[None, :]).astype(BF16),
    }


def _run_trunk(x, p, st_sc, st_mc, st_ssm, st_cf, st_ff, layers, g_final, tiles):
    h = x
    outs = [[], [], [], [], []]
    for i, w in enumerate(layers):
        n, mac, gb, nsc, ncf = _ac_call(h, st_sc[i], st_cf[i], w, *tiles['ac'])
        h1, nmc, nssm = _ssm_call(n, h, mac, gb, st_mc[i], st_ssm[i], w, *tiles['ssm'])
        h, nff = _ffn_call(h1, p[i], st_ff[i], w, g_final, *tiles['ffn'], final=(i == len(layers) - 1))
        for lst, s in zip(outs, (nsc, nmc, nssm, ncf, nff)):
            lst.append(s)
    return h, [jnp.stack(lst) for lst in outs]


def _tiles_for(bsz, tlen):
    if tlen >= 256:
        return {'ac': (1, 256), 'ssm': (1, 256), 'ffn': (1, 256)}
    return {'ac': (min(bsz, 16), tlen), 'ssm': (min(bsz, 4), tlen), 'ffn': (min(bsz, 16), tlen)}


def kernel(x_prompt, x_sample, p_prompt, p_sample, state_short_conv, state_ssm_conv, state_ssm, state_cf_conv, state_ffn_conv, g_mix, w_in, sc_conv_w, w_sc_out, ssm_conv_w, ssm_conv_b, ssm_dt_bias, ssm_a_log, ssm_d, ssm_norm_g, w_ssm_out, cf_conv_w, cf_conv_b, cf_ln_g, cf_ln_b, w_cf_out, w_o, g_ffn, w_up, ff_conv_w, ff_conv_b, w_down, g_ple, w_ple_gate, w_ple_proj, g_final):
    layers = [_prep_layer(i, g_mix, w_in, sc_conv_w, w_sc_out, ssm_conv_w, ssm_conv_b, ssm_dt_bias, ssm_a_log,
                          ssm_d, ssm_norm_g, w_ssm_out, cf_conv_w, cf_conv_b, cf_ln_g, cf_ln_b, w_cf_out, w_o,
                          g_ffn, w_up, ff_conv_w, ff_conv_b, w_down, g_ple, w_ple_gate, w_ple_proj)
              for i in range(DEPTH)]
    gfin = g_final.reshape(1, D_MODEL)
    bp, tp, _ = x_prompt.shape
    bs, ts, _ = x_sample.shape
    z_sc = jnp.zeros((DEPTH, bp, SC_KERNEL - 1, D_MODEL), F32)
    z_mc = jnp.zeros((DEPTH, bp, SSM_CONV - 1, SSM_CONV_DIM), F32)
    z_ssm = jnp.zeros((DEPTH, bp, SSM_HEADS, SSM_HEAD_DIM, SSM_STATE), F32)
    z_cf = jnp.zeros((DEPTH, bp, CF_KERNEL - 1, D_MODEL), F32)
    z_ff = jnp.zeros((DEPTH, bp, FF_KERNEL - 1, 2 * FF_DIM), F32)
    y_p, sp = _run_trunk(x_prompt, p_prompt, z_sc, z_mc, z_ssm, z_cf, z_ff, layers, gfin, _tiles_for(bp, tp))
    y_s, ss = _run_trunk(x_sample, p_sample, state_short_conv, state_ssm_conv, state_ssm, state_cf_conv,
                         state_ffn_conv, layers, gfin, _tiles_for(bs, ts))
    return (y_p, y_s, sp[0], sp[1], sp[2], sp[3], sp[4], ss[0], ss[1], ss[2], ss[3], ss[4])
```

```python
import functools

import jax
import jax.numpy as jnp
from jax import lax
from jax.experimental import pallas as pl
from jax.experimental.pallas import tpu as pltpu

F32 = jnp.float32
BF16 = jnp.bfloat16

D_MODEL = 1024
DEPTH = 2
PLE_DIM = 256
EPS = 1e-6
SC_KERNEL = 3
SSM_INNER = 2 * D_MODEL
SSM_HEAD_DIM = 64
SSM_HEADS = SSM_INNER // SSM_HEAD_DIM
SSM_GROUPS = 4
SSM_STATE = 128
SSM_CONV = 4
SSM_GN = SSM_GROUPS * SSM_STATE
SSM_CONV_DIM = SSM_INNER + 2 * SSM_GN
CF_KERNEL = 31
FF_DIM = 2816
FF_KERNEL = 3
OFF_SC = 3 * D_MODEL
OFF_Z = OFF_SC + 3 * D_MODEL
OFF_XBC = OFF_Z + SSM_INNER
OFF_DT = OFF_XBC + SSM_CONV_DIM
OFF_CF = OFF_DT + SSM_HEADS
N_IN = OFF_CF + 2 * D_MODEL

LANES = 128
SUBLANES = 8
MXU_DIM = 256
SSD_CHUNK = 128
GROUP_W = SSM_INNER // SSM_GROUPS
SPLIT_TERMS = 3
SLAB = 2 * MXU_DIM
AC_STEPS = 4
VMEM_LIMIT = 56 * 1024 * 1024


def _dot(a, b):
    return jnp.dot(a, b, preferred_element_type=F32)


def _sigmoid(x):
    return 1.0 / (1.0 + jnp.exp(-x))


def _silu(x):
    return x * _sigmoid(x)


def _softplus(x):
    return jnp.maximum(x, 0.0) + jnp.log1p(jnp.exp(-jnp.abs(x)))


def _rms(x, g):
    return x * lax.rsqrt(jnp.mean(x * x, axis=-1, keepdims=True) + EPS) * g


def _pad_rows(k):
    return -(-(k - 1) // SUBLANES) * SUBLANES


def _chunk_pos(i, cpb, rc):
    return i // cpb, (i % cpb) * rc


def _for_each(n, body):
    for i in range(n):
        body(i)


def _slabs(dst, lhs_fn, w_ref):
    thunks = []
    for lo in range(0, w_ref.shape[1], SLAB):
        cs = slice(lo, min(lo + SLAB, w_ref.shape[1]))

        def run(cs=cs):
            dst[:, cs] = _dot(lhs_fn(), w_ref[:, cs])

        thunks.append(run)
    return thunks


def _interleave(n_steps, step, thunks):
    done = 0
    for i in range(n_steps):
        upto = (len(thunks) * (i + 1)) // n_steps
        for th in thunks[done:upto]:
            th()
        done = upto
        step(i)


def _lane_blocks(c):
    return [slice(lb * LANES, (lb + 1) * LANES) for lb in range(c // LANES)]


def _load_conv_tail(xpad, pad, k, st_ref):
    for lb, ls in enumerate(_lane_blocks(st_ref.shape[-1])):
        xpad[:, lb, pad - (k - 1):pad, :] = st_ref[:, :, ls]


def _carry_conv_tail(xpad, pad, k, tt, new_ref, is_last):
    tail = xpad[:, :, pad + tt - (k - 1):pad + tt, :]

    @pl.when(is_last)
    def _():
        for lb, ls in enumerate(_lane_blocks(new_ref.shape[-1])):
            new_ref[:, :, ls] = tail[:, lb]

    xpad[:, :, pad - (k - 1):pad, :] = tail


def _store_tile(xpad, pad, val, nb, tt, first_block=0):
    for lb, ls in enumerate(_lane_blocks(val.shape[-1])):
        xpad[:, first_block + lb, pad:pad + tt, :] = val[:, ls].reshape(nb, tt, LANES)


def _ac_kernel(h_ref, stsc_ref, stcf_ref, gmix_ref, wag_ref, wcf_ref, scw_ref, wsco_ref,
               cfw_ref, cfb_ref, lng_ref, lnb_ref, wcfo_ref,
               n_ref, mac_ref, gb_ref, nsc_ref, ncf_ref,
               xsc, xcf, proj_ag, proj_c, conv_c, lhs_a, lhs_c, ya, yc, *, nb, tt, rc):
    t = pl.program_id(1)
    is_last = t == pl.num_programs(1) - 1
    rows = nb * tt
    cpb = tt // rc
    chunks = rows // rc
    psc, pcf = _pad_rows(SC_KERNEL), _pad_rows(CF_KERNEL)
    n_slabs = wag_ref.shape[0]
    steps = AC_STEPS
    cps, sps = chunks // steps, n_slabs // steps

    @pl.when(t == 0)
    def _():
        _load_conv_tail(xsc, psc, SC_KERNEL, stsc_ref)
        _load_conv_tail(xcf, pcf, CF_KERNEL, stcf_ref)

    h = h_ref[...].reshape(rows, D_MODEL)
    n_ref[...] = _rms(h, gmix_ref[...]).astype(BF16).reshape(nb, tt, D_MODEL)

    def n_tile():
        return n_ref[...].reshape(rows, D_MODEL)

    def ag(rs, col, width=LANES):
        return proj_ag[col // SLAB, rs, col % SLAB:col % SLAB + width]

    proj_c[...] = _dot(n_tile(), wcf_ref[...])

    def c_fill(i):
        b, r = _chunk_pos(i, cpb, rc)
        r2 = i * rc
        for lb, ls in enumerate(_lane_blocks(D_MODEL)):
            c_g = proj_c[pl.ds(r2, rc), D_MODEL + lb * LANES:D_MODEL + (lb + 1) * LANES]
            xcf[b, lb, pl.ds(pcf + r, rc), :] = proj_c[pl.ds(r2, rc), ls] * _sigmoid(c_g)

    _for_each(chunks, c_fill)

    def c_step(s, carry):
        for q in range(sps):
            proj_ag[s * sps + q] = _dot(n_tile(), wag_ref[s * sps + q])
        for lb, ls in enumerate(_lane_blocks(D_MODEL)):
            taps = [jnp.broadcast_to(cfw_ref[k:k + 1, ls], (SUBLANES, LANES)) for k in range(CF_KERNEL)]
            for q in range(cps):
                i = s * cps + q
                b, r = _chunk_pos(i, cpb, rc)
                for rg in range(0, rc, SUBLANES):
                    acc = None
                    for k in range(CF_KERNEL):
                        term = xcf[b, lb, pl.ds(pcf - (CF_KERNEL - 1) + k + r + rg, SUBLANES), :] * taps[k]
                        acc = term if acc is None else acc + term
                    conv_c[pl.ds(pl.multiple_of(i * rc + rg, SUBLANES), SUBLANES), ls] = acc
        for q in range(cps):
            r2 = pl.multiple_of((s * cps + q) * rc, rc)
            v = conv_c[pl.ds(r2, rc), :] + cfb_ref[...]
            mu = jnp.mean(v, axis=-1, keepdims=True)
            vc = v - mu
            var = jnp.mean(vc * vc, axis=-1, keepdims=True)
            ln = vc * lax.rsqrt(var + EPS) * lng_ref[...] + lnb_ref[...]
            lhs_c[pl.ds(r2, rc), :] = _silu(ln).astype(BF16)
        return carry

    lax.fori_loop(0, steps, c_step, 0)
    _carry_conv_tail(xcf, pcf, CF_KERNEL, tt, ncf_ref, is_last)

    def a_fill(i):
        b, r = _chunk_pos(i, cpb, rc)
        rs = pl.ds(i * rc, rc)
        for lb, ls in enumerate(_lane_blocks(D_MODEL)):
            xsc[b, lb, pl.ds(psc + r, rc), :] = (ag(rs, OFF_SC + D_MODEL + lb * LANES)
                                                 * ag(rs, OFF_SC + 2 * D_MODEL + lb * LANES))

    _for_each(chunks, a_fill)

    def a_conv(i):
        b, r = _chunk_pos(i, cpb, rc)
        rs = pl.ds(i * rc, rc)
        for lb, ls in enumerate(_lane_blocks(D_MODEL)):
            u = None
            for k in range(SC_KERNEL):
                term = xsc[b, lb, pl.ds(psc - (SC_KERNEL - 1) + k + r, rc), :] * scw_ref[k:k + 1, ls]
                u = term if u is None else u + term
            lhs_a[rs, ls] = (ag(rs, OFF_SC + lb * LANES) * u).astype(BF16)

    _for_each(chunks, a_conv)
    _carry_conv_tail(xsc, psc, SC_KERNEL, tt, nsc_ref, is_last)
    yc[...] = _dot(lhs_c[...], wcfo_ref[...])
    ya[...] = _dot(lhs_a[...], wsco_ref[...])

    def g_merge(i):
        b, r = _chunk_pos(i, cpb, rc)
        rs = pl.ds(i * rc, rc)
        for lo in range(0, D_MODEL, SLAB):
            cs = slice(lo, lo + SLAB)
            g_a = _sigmoid(ag(rs, lo, SLAB))
            g_b = _sigmoid(ag(rs, D_MODEL + lo, SLAB))
            g_c = _sigmoid(ag(rs, 2 * D_MODEL + lo, SLAB))
            mac_ref[b, pl.ds(r, rc), cs] = g_a * ya[rs, cs] + g_c * yc[rs, cs]
            gb_ref[b, pl.ds(r, rc), cs] = g_b

    _for_each(chunks, g_merge)


def _ssm_kernel(n_ref, h_ref, mac_ref, gb_ref, stmc_ref, stssm_ref,
                wz_ref, wxbc_ref, wdt_ref, cw_ref, cb_ref, dtb_ref, alog_ref, de_ref, ng_ref,
                wso_ref, wo_ref, e_ref,
                h1_ref, nmc_ref, nssm_ref,
                xpad, zs, xs_s, bs_s, cs_s, dt_s, acs_s, acst_s, ht, ys, *, nb, tt, rc):
    t = pl.program_id(1)
    is_last = t == pl.num_programs(1) - 1
    rows = nb * tt
    cpb = tt // rc
    pmc = _pad_rows(SSM_CONV)
    L = SSD_CHUNK
    seq_mode = tt >= L
    lv = L if seq_mode else tt
    n_chunks = rows // lv

    @pl.when(t == 0)
    def _():
        _load_conv_tail(xpad, pmc, SSM_CONV, stmc_ref)

    def load_state(b):
        for j in range(SSM_INNER // LANES):
            ht[:, j * LANES:(j + 1) * LANES] = stssm_ref[b, j * LANES:(j + 1) * LANES, :].T

    def store_state(b):
        for j in range(SSM_INNER // LANES):
            nssm_ref[b, j * LANES:(j + 1) * LANES, :] = ht[:, j * LANES:(j + 1) * LANES].T

    if seq_mode:
        @pl.when(t == 0)
        def _():
            load_state(0)
    else:
        xs_s[...] = jnp.zeros_like(xs_s)
        bs_s[...] = jnp.zeros_like(bs_s)
        cs_s[...] = jnp.zeros_like(cs_s)
        dt_s[...] = jnp.zeros_like(dt_s)

    def n_tile():
        return n_ref[...].reshape(rows, D_MODEL)

    _store_tile(xpad, pmc, _dot(n_tile(), wxbc_ref[...]), nb, tt)
    dt_all = _softplus(_dot(n_tile(), wdt_ref[...]) + dtb_ref[...])
    if seq_mode:
        dt_s[...] = dt_all
    else:
        for b in range(nb):
            dt_s[b * L:b * L + tt, :] = dt_all[b * tt:(b + 1) * tt, :]

    def conv(i):
        b, r = _chunk_pos(i, cpb, rc)
        dst = i * (rc if seq_mode else L)
        for lb, ls in enumerate(_lane_blocks(SSM_CONV_DIM)):
            acc = None
            for k in range(SSM_CONV):
                term = xpad[b, lb, pl.ds(pmc - (SSM_CONV - 1) + k + r, rc), :] * cw_ref[k:k + 1, ls]
                acc = term if acc is None else acc + term
            xbc = _silu(acc + cb_ref[:, ls])
            if ls.start < SSM_INNER:
                xs_s[pl.ds(dst, rc), ls] = xbc
            elif ls.start < SSM_INNER + SSM_GN:
                bs_s[pl.ds(dst, rc), ls.start - SSM_INNER:ls.stop - SSM_INNER] = xbc
            else:
                cs_s[pl.ds(dst, rc), ls.start - SSM_INNER - SSM_GN:ls.stop - SSM_INNER - SSM_GN] = xbc

    _interleave(rows // rc, conv, _slabs(zs, n_tile, wz_ref))
    _carry_conv_tail(xpad, pmc, SSM_CONV, tt, nmc_ref, is_last)

    a_row = -jnp.exp(alog_ref[...])
    ii = lax.broadcasted_iota(jnp.int32, (L, L), 0)
    jj = lax.broadcasted_iota(jnp.int32, (L, L), 1)
    causal = ii >= jj
    tril = jnp.where(causal, 1.0, 0.0).astype(BF16)
    low_half = (lax.broadcasted_iota(jnp.int32, (L, GROUP_W), 1) % LANES) < SSM_HEAD_DIM

    def pack_terms(x):
        head_lane = lax.broadcasted_iota(jnp.int32, x.shape, 1) < SSM_HEADS
        r = jnp.where(head_lane, x, 0.0)
        packed = None
        for k in range(SPLIT_TERMS):
            p = r.astype(BF16).astype(F32)
            placed = p if k == 0 else pltpu.roll(p, k * SSM_HEADS, 1)
            packed = placed if packed is None else packed + placed
            r = r - p
        return packed.astype(BF16)

    def expand(x):
        return _dot(pack_terms(x), e_ref[...])

    def chunk(c):
        if not seq_mode:
            load_state(c)
        r0 = c * L
        rv = c * lv
        dtc = dt_s[pl.ds(r0, L), :]
        cs3 = _dot(tril, pack_terms(dtc * a_row))
        acs = cs3
        for k in range(1, SPLIT_TERMS):
            acs = acs + pltpu.roll(cs3, LANES - k * SSM_HEADS, 1)
        acs_s[...] = acs
        acst_s[...] = acs.T
        a_last = acs[L - 1:L, :]
        dt_e = expand(dtc)
        eacs_e = expand(jnp.exp(acs))
        dec_e = expand(jnp.exp(a_last - acs))
        elast_e = expand(jnp.broadcast_to(jnp.exp(a_last), (SUBLANES, LANES)))[0:1, :]
        for g in range(SSM_GROUPS):
            gs = slice(g * GROUP_W, (g + 1) * GROUP_W)
            ns = slice(g * SSM_STATE, (g + 1) * SSM_STATE)
            xs_g = xs_s[pl.ds(r0, L), gs]
            xdt = xs_g * dt_e[:, gs]
            x_even = jnp.where(low_half, xdt, 0.0).astype(BF16)
            x_odd = jnp.where(low_half, 0.0, xdt).astype(BF16)
            xdw_b = (xdt * dec_e[:, gs]).astype(BF16)
            c_g = cs_s[pl.ds(r0, L), ns].astype(BF16)
            bt_g = bs_s[pl.ds(r0, L), ns].T.astype(BF16)
            cb = _dot(c_g, bt_g)
            ht_g = ht[:, gs]
            y_off = _dot(c_g, ht_g.astype(BF16)) * eacs_e[:, gs]
            ht[:, gs] = ht_g * elast_e[:, gs] + _dot(bt_g, xdw_b)
            y_cols = []
            for pr in range(GROUP_W // LANES):
                hd = g * (GROUP_W // SSM_HEAD_DIM) + 2 * pr
                ps = slice(pr * LANES, (pr + 1) * LANES)
                ms = []
                for hh in (hd, hd + 1):
                    diff = acs_s[:, hh:hh + 1] - acst_s[hh:hh + 1, :]
                    ms.append((cb * jnp.exp(jnp.where(causal, diff, -jnp.inf))).astype(BF16))
                y_cols.append(_dot(jnp.concatenate(ms, axis=1),
                                   jnp.concatenate([x_even[:, ps], x_odd[:, ps]], axis=0)))
            y = jnp.concatenate(y_cols, axis=-1) + y_off + de_ref[:, gs] * xs_g
            y = y[0:lv, :] * _silu(zs[pl.ds(rv, lv), gs])
            y = y * lax.rsqrt(jnp.mean(y * y, axis=-1, keepdims=True) + EPS)
            ys[pl.ds(rv, lv), gs] = (y * ng_ref[:, gs]).astype(BF16)
        if not seq_mode:
            store_state(c)

    def finish(b, r, m):
        y_b = _dot(ys[pl.ds(b * tt + r, m), :], wso_ref[...])
        merged = mac_ref[b, pl.ds(r, m), :] + gb_ref[b, pl.ds(r, m), :] * y_b
        h1_ref[b, pl.ds(r, m), :] = h_ref[b, pl.ds(r, m), :] + _dot(merged.astype(BF16), wo_ref[...])

    for c in range(n_chunks):
        chunk(c)
        if seq_mode:
            finish(0, c * L, L)

    if seq_mode:
        @pl.when(is_last)
        def _():
            store_state(0)
    else:
        y_b = _dot(ys[...], wso_ref[...])
        merged = mac_ref[...].reshape(rows, D_MODEL) + gb_ref[...].reshape(rows, D_MODEL) * y_b
        h1 = h_ref[...].reshape(rows, D_MODEL) + _dot(merged.astype(BF16), wo_ref[...])
        h1_ref[...] = h1.reshape(nb, tt, D_MODEL)


def _ffn_kernel(h_ref, p_ref, stff_ref, gffn_ref, wup_ref, fw_ref, fb_ref, wdn_ref, gple_ref,
                wpg_ref, wpp_ref, gfin_ref,
                out_ref, nff_ref,
                xpad, nf_s, lhs, *, nb, tt, rc, final):
    t = pl.program_id(1)
    is_last = t == pl.num_programs(1) - 1
    rows = nb * tt
    cpb = tt // rc
    pff = _pad_rows(FF_KERNEL)
    half_blocks = FF_DIM // LANES
    slab_blocks = MXU_DIM // LANES
    n_slabs = FF_DIM // MXU_DIM

    @pl.when(t == 0)
    def _():
        _load_conv_tail(xpad, pff, FF_KERNEL, stff_ref)

    h1 = h_ref[...].reshape(rows, D_MODEL)
    nf_s[...] = _rms(h1, gffn_ref[...]).astype(BF16)

    def up_slab(j):
        for half in range(2):
            lo = half * FF_DIM + j * MXU_DIM
            _store_tile(xpad, pff, _dot(nf_s[...], wup_ref[:, lo:lo + MXU_DIM]), nb, tt, first_block=lo // LANES)

    def conv_slab(j):
        for i in range(rows // rc):
            b, r = _chunk_pos(i, cpb, rc)
            r2 = i * rc
            for lb in range(j * slab_blocks, (j + 1) * slab_blocks):
                halves = []
                for blk in (lb, half_blocks + lb):
                    cs = slice(blk * LANES, (blk + 1) * LANES)
                    acc = None
                    for k in range(FF_KERNEL):
                        term = xpad[b, blk, pl.ds(pff - (FF_KERNEL - 1) + k + r, rc), :] * fw_ref[k:k + 1, cs]
                        acc = term if acc is None else acc + term
                    halves.append(acc + fb_ref[:, cs])
                lhs[pl.ds(r2, rc), lb * LANES:(lb + 1) * LANES] = (_silu(halves[0]) * halves[1]).astype(BF16)

    up_slab(0)
    for j in range(n_slabs):
        if j + 1 < n_slabs:
            up_slab(j + 1)
        conv_slab(j)
    _carry_conv_tail(xpad, pff, FF_KERNEL, tt, nff_ref, is_last)

    h2 = h1 + _dot(lhs[...], wdn_ref[...])
    gate = _sigmoid(_dot(_rms(h2, gple_ref[...]).astype(BF16), wpg_ref[...]))
    pp = _dot(p_ref[...].reshape(rows, PLE_DIM).astype(BF16), wpp_ref[...])
    h3 = h2 + pp * gate
    if final:
        h3 = _rms(h3, gfin_ref[...])
    out_ref[...] = h3.reshape(nb, tt, D_MODEL)


def _tile_spec(nb, tt, c):
    return pl.BlockSpec((nb, tt, c), lambda b, t: (b, t, 0))


def _state_spec(nb, r, c):
    return pl.BlockSpec((nb, r, c), lambda b, t: (b, 0, 0))


def _const_spec(arr):
    return pl.BlockSpec(arr.shape, lambda b, t: (0,) * arr.ndim, pipeline_mode=pl.Buffered(1))


def _params():
    return pltpu.CompilerParams(dimension_semantics=("arbitrary", "arbitrary"),
                                vmem_limit_bytes=VMEM_LIMIT)


def _conv_buffer(nb, k, tt, c):
    return pltpu.VMEM((nb, c // LANES, _pad_rows(k) + tt, LANES), F32)


def _ac_call(h, st_sc, st_cf, w, nb, tt):
    bsz, tlen, _ = h.shape
    rc = min(tt, 32)
    rows = nb * tt
    consts = [w['g_mix'], w['w_ag'], w['w_cf'], w['sc_conv_w'], w['w_sc_out'],
              w['cf_conv_w'], w['cf_conv_b'], w['cf_ln_g'], w['cf_ln_b'], w['w_cf_out']]
    return pl.pallas_call(
        functools.partial(_ac_kernel, nb=nb, tt=tt, rc=rc),
        grid=(bsz // nb, tlen // tt),
        in_specs=[_tile_spec(nb, tt, D_MODEL), _state_spec(nb, SC_KERNEL - 1, D_MODEL),
                  _state_spec(nb, CF_KERNEL - 1, D_MODEL)] + [_const_spec(a) for a in consts],
        out_specs=[_tile_spec(nb, tt, D_MODEL), _tile_spec(nb, tt, D_MODEL), _tile_spec(nb, tt, D_MODEL),
                   _state_spec(nb, SC_KERNEL - 1, D_MODEL), _state_spec(nb, CF_KERNEL - 1, D_MODEL)],
        out_shape=[jax.ShapeDtypeStruct((bsz, tlen, D_MODEL), BF16),
                   jax.ShapeDtypeStruct((bsz, tlen, D_MODEL), F32),
                   jax.ShapeDtypeStruct((bsz, tlen, D_MODEL), F32),
                   jax.ShapeDtypeStruct((bsz, SC_KERNEL - 1, D_MODEL), F32),
                   jax.ShapeDtypeStruct((bsz, CF_KERNEL - 1, D_MODEL), F32)],
        scratch_shapes=[_conv_buffer(nb, SC_KERNEL, tt, D_MODEL),
                        _conv_buffer(nb, CF_KERNEL, tt, D_MODEL),
                        pltpu.VMEM((OFF_Z // SLAB, rows, SLAB), F32),
                        pltpu.VMEM((rows, 2 * D_MODEL), F32),
                        pltpu.VMEM((rows, D_MODEL), F32),
                        pltpu.VMEM((rows, D_MODEL), BF16),
                        pltpu.VMEM((rows, D_MODEL), BF16),
                        pltpu.VMEM((rows, D_MODEL), F32),
                        pltpu.VMEM((rows, D_MODEL), F32)],
        compiler_params=_params(),
        name="mix_ac",
    )(h, st_sc, st_cf, *consts)


def _ssm_call(n, h, mac, gb, st_mc, st_ssm, w, nb, tt):
    bsz, tlen, _ = h.shape
    rc = min(tt, 32)
    rows = nb * tt
    L = SSD_CHUNK
    prow = rows if tt >= L else nb * L
    consts = [w['w_z'], w['w_xbc'], w['w_dt'], w['ssm_conv_w'], w['ssm_conv_b'], w['dt_bias'], w['a_log'],
              w['d_exp'], w['ssm_norm_g'], w['w_ssm_out'], w['w_o'], w['head_expand']]
    st_ssm2 = st_ssm.reshape(bsz, SSM_INNER, SSM_STATE)
    h1, nmc, nssm = pl.pallas_call(
        functools.partial(_ssm_kernel, nb=nb, tt=tt, rc=rc),
        grid=(bsz // nb, tlen // tt),
        in_specs=[_tile_spec(nb, tt, D_MODEL)] * 4
        + [_state_spec(nb, SSM_CONV - 1, SSM_CONV_DIM), _state_spec(nb, SSM_INNER, SSM_STATE)]
        + [_const_spec(a) for a in consts],
        out_specs=[_tile_spec(nb, tt, D_MODEL), _state_spec(nb, SSM_CONV - 1, SSM_CONV_DIM),
                   _state_spec(nb, SSM_INNER, SSM_STATE)],
        out_shape=[jax.ShapeDtypeStruct((bsz, tlen, D_MODEL), F32),
                   jax.ShapeDtypeStruct((bsz, SSM_CONV - 1, SSM_CONV_DIM), F32),
                   jax.ShapeDtypeStruct((bsz, SSM_INNER, SSM_STATE), F32)],
        scratch_shapes=[_conv_buffer(nb, SSM_CONV, tt, SSM_CONV_DIM),
                        pltpu.VMEM((rows, SSM_INNER), F32),
                        pltpu.VMEM((prow, SSM_INNER), F32),
                        pltpu.VMEM((prow, SSM_GN), F32),
                        pltpu.VMEM((prow, SSM_GN), F32),
                        pltpu.VMEM((prow, LANES), F32),
                        pltpu.VMEM((L, LANES), F32),
                        pltpu.VMEM((LANES, L), F32),
                        pltpu.VMEM((SSM_STATE, SSM_INNER), F32),
                        pltpu.VMEM((rows, SSM_INNER), BF16)],
        compiler_params=_params(),
        name="mix_ssm",
    )(n, h, mac, gb, st_mc, st_ssm2, *consts)
    return h1, nmc, nssm.reshape(bsz, SSM_HEADS, SSM_HEAD_DIM, SSM_STATE)


def _ffn_call(h1, p, st_ff, w, g_final, nb, tt, final):
    bsz, tlen, _ = h1.shape
    rc = min(tt, 32)
    rows = nb * tt
    consts = [w['g_ffn'], w['w_up'], w['ff_conv_w'], w['ff_conv_b'], w['w_down'], w['g_ple'],
              w['w_ple_gate'], w['w_ple_proj'], g_final]
    return pl.pallas_call(
        functools.partial(_ffn_kernel, nb=nb, tt=tt, rc=rc, final=final),
        grid=(bsz // nb, tlen // tt),
        in_specs=[_tile_spec(nb, tt, D_MODEL), _tile_spec(nb, tt, PLE_DIM),
                  _state_spec(nb, FF_KERNEL - 1, 2 * FF_DIM)] + [_const_spec(a) for a in consts],
        out_specs=[_tile_spec(nb, tt, D_MODEL), _state_spec(nb, FF_KERNEL - 1, 2 * FF_DIM)],
        out_shape=[jax.ShapeDtypeStruct((bsz, tlen, D_MODEL), F32),
                   jax.ShapeDtypeStruct((bsz, FF_KERNEL - 1, 2 * FF_DIM), F32)],
        scratch_shapes=[_conv_buffer(nb, FF_KERNEL, tt, 2 * FF_DIM),
                        pltpu.VMEM((rows, D_MODEL), BF16),
                        pltpu.VMEM((rows, FF_DIM), BF16)],
        compiler_params=_params(),
        name="ffn_ple",
    )(h1, p, st_ff, *consts)


def _prep_layer(i, g_mix, w_in, sc_conv_w, w_sc_out, ssm_conv_w, ssm_conv_b, ssm_dt_bias, ssm_a_log, ssm_d,
                ssm_norm_g, w_ssm_out, cf_conv_w, cf_conv_b, cf_ln_g, cf_ln_b, w_cf_out, w_o, g_ffn, w_up,
                ff_conv_w, ff_conv_b, w_down, g_ple, w_ple_gate, w_ple_proj):
    wi = w_in[i]
    row = lambda v: v[i].reshape(1, -1)
    lane_pad = lambda v: jnp.pad(v, [(0, 0)] * (v.ndim - 1) + [(0, LANES - v.shape[-1])])
    head_of_channel = jnp.arange(SSM_INNER) // SSM_HEAD_DIM
    term_lane = jnp.arange(LANES)
    return {
        'g_mix': row(g_mix),
        'w_ag': wi[:, 0:OFF_Z].astype(BF16).reshape(D_MODEL, OFF_Z // SLAB, SLAB).transpose(1, 0, 2),
        'w_z': wi[:, OFF_Z:OFF_XBC].astype(BF16),
        'w_xbc': wi[:, OFF_XBC:OFF_DT].astype(BF16),
        'w_dt': lane_pad(wi[:, OFF_DT:OFF_CF]).astype(BF16),
        'w_cf': wi[:, OFF_CF:N_IN].astype(BF16),
        'sc_conv_w': sc_conv_w[i],
        'w_sc_out': w_sc_out[i].astype(BF16),
        'ssm_conv_w': ssm_conv_w[i],
        'ssm_conv_b': row(ssm_conv_b),
        'dt_bias': lane_pad(row(ssm_dt_bias)),
        'a_log': lane_pad(row(ssm_a_log)),
        'd_exp': jnp.repeat(ssm_d[i], SSM_HEAD_DIM).reshape(1, SSM_INNER),
        'ssm_norm_g': row(ssm_norm_g),
        'w_ssm_out': w_ssm_out[i].astype(BF16),
        'cf_conv_w': cf_conv_w[i],
        'cf_conv_b': row(cf_conv_b),
        'cf_ln_g': row(cf_ln_g),
        'cf_ln_b': row(cf_ln_b),
        'w_cf_out': w_cf_out[i].astype(BF16),
        'w_o': w_o[i].astype(BF16),
        'g_ffn': row(g_ffn),
        'w_up': w_up[i].astype(BF16),
        'ff_conv_w': ff_conv_w[i],
        'ff_conv_b': row(ff_conv_b),
        'w_down': w_down[i].astype(BF16),
        'g_ple': row(g_ple),
        'w_ple_gate': w_ple_gate[i].astype(BF16),
        'w_ple_proj': w_ple_proj[i].astype(BF16),
        'head_expand': ((term_lane[:, None] % SSM_HEADS == head_of_channel[None, :])
                        & (term_lane[:, None] < SPLIT_TERMS * SSM_HEADS)).astype(BF16),
    }


def _run_trunk(x, p, st_sc, st_mc, st_ssm, st_cf, st_ff, layers, g_final, tiles):
    h = x
    outs = [[], [], [], [], []]
    for i, w in enumerate(layers):
        n, mac, gb, nsc, ncf = _ac_call(h, st_sc[i], st_cf[i], w, *tiles['ac'])
        h1, nmc, nssm = _ssm_call(n, h, mac, gb, st_mc[i], st_ssm[i], w, *tiles['ssm'])
        h, nff = _ffn_call(h1, p[i], st_ff[i], w, g_final, *tiles['ffn'], final=(i == len(layers) - 1))
        for lst, s in zip(outs, (nsc, nmc, nssm, ncf, nff)):
            lst.append(s)
    return h, [jnp.stack(lst) for lst in outs]


def _tiles_for(bsz, tlen):
    if tlen >= 256:
        return {'ac': (1, 256), 'ssm': (1, 256), 'ffn': (1, 256)}
    return {'ac': (min(bsz, 16), tlen), 'ssm': (min(bsz, 4), tlen), 'ffn': (min(bsz, 16), tlen)}


def kernel(x_prompt, x_sample, p_prompt, p_sample, state_short_conv, state_ssm_conv, state_ssm, state_cf_conv, state_ffn_conv, g_mix, w_in, sc_conv_w, w_sc_out, ssm_conv_w, ssm_conv_b, ssm_dt_bias, ssm_a_log, ssm_d, ssm_norm_g, w_ssm_out, cf_conv_w, cf_conv_b, cf_ln_g, cf_ln_b, w_cf_out, w_o, g_ffn, w_up, ff_conv_w, ff_conv_b, w_down, g_ple, w_ple_gate, w_ple_proj, g_final):
    layers = [_prep_layer(i, g_mix, w_in, sc_conv_w, w_sc_out, ssm_conv_w, ssm_conv_b, ssm_dt_bias, ssm_a_log,
                          ssm_d, ssm_norm_g, w_ssm_out, cf_conv_w, cf_conv_b, cf_ln_g, cf_ln_b, w_cf_out, w_o,
                          g_ffn, w_up, ff_conv_w, ff_conv_b, w_down, g_ple, w_ple_gate, w_ple_proj)
              for i in range(DEPTH)]
    gfin = g_final.reshape(1, D_MODEL)
    bp, tp, _ = x_prompt.shape
    bs, ts, _ = x_sample.shape
    z_sc = jnp.zeros((DEPTH, bp, SC_KERNEL - 1, D_MODEL), F32)
    z_mc = jnp.zeros((DEPTH, bp, SSM_CONV - 1, SSM_CONV_DIM), F32)
    z_ssm = jnp.zeros((DEPTH, bp, SSM_HEADS, SSM_HEAD_DIM, SSM_STATE), F32)
    z_cf = jnp.zeros((DEPTH, bp, CF_KERNEL - 1, D_MODEL), F32)
    z_ff = jnp.zeros((DEPTH, bp, FF_KERNEL - 1, 2 * FF_DIM), F32)
    y_p, sp = _run_trunk(x_prompt, p_prompt, z_sc, z_mc, z_ssm, z_cf, z_ff, layers, gfin, _tiles_for(bp, tp))
    y_s, ss = _run_trunk(x_sample, p_sample, state_short_conv, state_ssm_conv, state_ssm, state_cf_conv,
                         state_ffn_conv, layers, gfin, _tiles_for(bs, ts))
    return (y_p, y_s, sp[0], sp[1], sp[2], sp[3], sp[4], ss[0], ss[1], ss[2], ss[3], ss[4])
```

```python
import functools

import jax
import jax.numpy as jnp
from jax import lax
from jax.experimental import pallas as pl
from jax.experimental.pallas import tpu as pltpu

F32 = jnp.float32
BF16 = jnp.bfloat16

D_MODEL = 1024
DEPTH = 2
PLE_DIM = 256
EPS = 1e-6
SC_KERNEL = 3
SSM_INNER = 2 * D_MODEL
SSM_HEAD_DIM = 64
SSM_HEADS = SSM_INNER // SSM_HEAD_DIM
SSM_GROUPS = 4
SSM_STATE = 128
SSM_CONV = 4
SSM_GN = SSM_GROUPS * SSM_STATE
SSM_CONV_DIM = SSM_INNER + 2 * SSM_GN
CF_KERNEL = 31
FF_DIM = 2816
FF_KERNEL = 3
OFF_SC = 3 * D_MODEL
OFF_Z = OFF_SC + 3 * D_MODEL
OFF_XBC = OFF_Z + SSM_INNER
OFF_DT = OFF_XBC + SSM_CONV_DIM
OFF_CF = OFF_DT + SSM_HEADS
N_IN = OFF_CF + 2 * D_MODEL

LANES = 128
SUBLANES = 8
MXU_DIM = 256
SSD_CHUNK = 128
GROUP_W = SSM_INNER // SSM_GROUPS
SPLIT_TERMS = 3
SLAB = 2 * MXU_DIM
AC_GROUP_AG = OFF_Z // (D_MODEL // LANES)
AC_GROUP_W = 2 * LANES + AC_GROUP_AG
VMEM_LIMIT = 56 * 1024 * 1024


def _dot(a, b):
    return jnp.dot(a, b, preferred_element_type=F32)


def _sigmoid(x):
    return 1.0 / (1.0 + jnp.exp(-x))


def _silu(x):
    return x * _sigmoid(x)


def _softplus(x):
    return jnp.maximum(x, 0.0) + jnp.log1p(jnp.exp(-jnp.abs(x)))


def _rms(x, g):
    return x * lax.rsqrt(jnp.mean(x * x, axis=-1, keepdims=True) + EPS) * g


def _pad_rows(k):
    return -(-(k - 1) // SUBLANES) * SUBLANES


def _chunk_pos(i, cpb, rc):
    return i // cpb, (i % cpb) * rc


def _for_each(n, body):
    for i in range(n):
        body(i)


def _slabs(dst, lhs_fn, w_ref):
    thunks = []
    for lo in range(0, w_ref.shape[1], SLAB):
        cs = slice(lo, min(lo + SLAB, w_ref.shape[1]))

        def run(cs=cs):
            dst[:, cs] = _dot(lhs_fn(), w_ref[:, cs])

        thunks.append(run)
    return thunks


def _interleave(n_steps, step, thunks):
    done = 0
    for i in range(n_steps):
        upto = (len(thunks) * (i + 1)) // n_steps
        for th in thunks[done:upto]:
            th()
        done = upto
        step(i)


def _lane_blocks(c):
    return [slice(lb * LANES, (lb + 1) * LANES) for lb in range(c // LANES)]


def _load_conv_tail(xpad, pad, k, st_ref):
    for lb, ls in enumerate(_lane_blocks(st_ref.shape[-1])):
        xpad[:, lb, pad - (k - 1):pad, :] = st_ref[:, :, ls]


def _carry_conv_tail(xpad, pad, k, tt, new_ref, is_last):
    tail = xpad[:, :, pad + tt - (k - 1):pad + tt, :]

    @pl.when(is_last)
    def _():
        for lb, ls in enumerate(_lane_blocks(new_ref.shape[-1])):
            new_ref[:, :, ls] = tail[:, lb]

    xpad[:, :, pad - (k - 1):pad, :] = tail


def _store_tile(xpad, pad, val, nb, tt, first_block=0):
    for lb, ls in enumerate(_lane_blocks(val.shape[-1])):
        xpad[:, first_block + lb, pad:pad + tt, :] = val[:, ls].reshape(nb, tt, LANES)


def _ac_kernel(h_ref, stsc_ref, stcf_ref, gmix_ref, wall_ref, scw_ref, wsco_ref,
               cfw_ref, cfb_ref, lng_ref, lnb_ref, wcfo_ref,
               n_ref, mac_ref, gb_ref, nsc_ref, ncf_ref,
               xsc, xcf, proj, conv_c, lhs_a, lhs_c, ya, yc, *, nb, tt, rc):
    t = pl.program_id(1)
    is_last = t == pl.num_programs(1) - 1
    rows = nb * tt
    cpb = tt // rc
    chunks = rows // rc
    psc, pcf = _pad_rows(SC_KERNEL), _pad_rows(CF_KERNEL)

    @pl.when(t == 0)
    def _():
        _load_conv_tail(xsc, psc, SC_KERNEL, stsc_ref)
        _load_conv_tail(xcf, pcf, CF_KERNEL, stcf_ref)

    h = h_ref[...].reshape(rows, D_MODEL)
    n = _rms(h, gmix_ref[...]).astype(BF16)
    n_ref[...] = n.reshape(nb, tt, D_MODEL)

    proj[...] = _dot(n, wall_ref[...])

    def cf(rs, half, lb):
        lo = lb * AC_GROUP_W + half * LANES
        return proj[rs, lo:lo + LANES]

    def ag(rs, col, width=LANES):
        lo = (col // AC_GROUP_AG) * AC_GROUP_W + 2 * LANES + col % AC_GROUP_AG
        return proj[rs, lo:lo + width]

    for lb, ls in enumerate(_lane_blocks(D_MODEL)):
        for i in range(chunks):
            b, r = _chunk_pos(i, cpb, rc)
            rs = pl.ds(i * rc, rc)
            xcf[b, lb, pl.ds(pcf + r, rc), :] = cf(rs, 0, lb) * _sigmoid(cf(rs, 1, lb))
        for i in range(chunks):
            b, r = _chunk_pos(i, cpb, rc)
            acc = None
            for k in range(CF_KERNEL):
                term = xcf[b, lb, pl.ds(pcf - (CF_KERNEL - 1) + k + r, rc), :] * cfw_ref[k:k + 1, ls]
                acc = term if acc is None else acc + term
            conv_c[pl.ds(i * rc, rc), ls] = acc
    _carry_conv_tail(xcf, pcf, CF_KERNEL, tt, ncf_ref, is_last)

    def c_norm(i):
        rs = pl.ds(i * rc, rc)
        v = conv_c[rs, :] + cfb_ref[...]
        mu = jnp.mean(v, axis=-1, keepdims=True)
        vc = v - mu
        var = jnp.mean(vc * vc, axis=-1, keepdims=True)
        ln = vc * lax.rsqrt(var + EPS) * lng_ref[...] + lnb_ref[...]
        lhs_c[rs, :] = _silu(ln).astype(BF16)

    _for_each(chunks, c_norm)
    yc[...] = _dot(lhs_c[...], wcfo_ref[...])

    def a_fill(i):
        b, r = _chunk_pos(i, cpb, rc)
        rs = pl.ds(i * rc, rc)
        for lb, ls in enumerate(_lane_blocks(D_MODEL)):
            xsc[b, lb, pl.ds(psc + r, rc), :] = (ag(rs, OFF_SC + D_MODEL + lb * LANES)
                                                 * ag(rs, OFF_SC + 2 * D_MODEL + lb * LANES))

    _for_each(chunks, a_fill)

    def a_conv(i):
        b, r = _chunk_pos(i, cpb, rc)
        rs = pl.ds(i * rc, rc)
        for lb, ls in enumerate(_lane_blocks(D_MODEL)):
            u = None
            for k in range(SC_KERNEL):
                term = xsc[b, lb, pl.ds(psc - (SC_KERNEL - 1) + k + r, rc), :] * scw_ref[k:k + 1, ls]
                u = term if u is None else u + term
            lhs_a[rs, ls] = (ag(rs, OFF_SC + lb * LANES) * u).astype(BF16)

    _for_each(chunks, a_conv)
    _carry_conv_tail(xsc, psc, SC_KERNEL, tt, nsc_ref, is_last)
    ya[...] = _dot(lhs_a[...], wsco_ref[...])

    def g_merge(i):
        b, r = _chunk_pos(i, cpb, rc)
        rs = pl.ds(i * rc, rc)
        for lo in range(0, D_MODEL, MXU_DIM):
            cs = slice(lo, lo + MXU_DIM)
            g_a = _sigmoid(ag(rs, lo, MXU_DIM))
            g_b = _sigmoid(ag(rs, D_MODEL + lo, MXU_DIM))
            g_c = _sigmoid(ag(rs, 2 * D_MODEL + lo, MXU_DIM))
            mac_ref[b, pl.ds(r, rc), cs] = g_a * ya[rs, cs] + g_c * yc[rs, cs]
            gb_ref[b, pl.ds(r, rc), cs] = g_b

    _for_each(chunks, g_merge)


def _ssm_kernel(n_ref, h_ref, mac_ref, gb_ref, stmc_ref, stssm_ref,
                wz_ref, wxbc_ref, wdt_ref, cw_ref, cb_ref, dtb_ref, alog_ref, de_ref, ng_ref,
                wso_ref, wo_ref, e_ref,
                h1_ref, nmc_ref, nssm_ref,
                xpad, zs, xs_s, bs_s, cs_s, dt_s, acs_s, acst_s, ht, ys, *, nb, tt, rc):
    t = pl.program_id(1)
    is_last = t == pl.num_programs(1) - 1
    rows = nb * tt
    cpb = tt // rc
    pmc = _pad_rows(SSM_CONV)
    L = SSD_CHUNK
    seq_mode = tt >= L
    lv = L if seq_mode else tt
    n_chunks = rows // lv

    @pl.when(t == 0)
    def _():
        _load_conv_tail(xpad, pmc, SSM_CONV, stmc_ref)

    def load_state(b):
        for j in range(SSM_INNER // LANES):
            ht[:, j * LANES:(j + 1) * LANES] = stssm_ref[b, j * LANES:(j + 1) * LANES, :].T

    def store_state(b):
        for j in range(SSM_INNER // LANES):
            nssm_ref[b, j * LANES:(j + 1) * LANES, :] = ht[:, j * LANES:(j + 1) * LANES].T

    if seq_mode:
        @pl.when(t == 0)
        def _():
            load_state(0)
    else:
        xs_s[...] = jnp.zeros_like(xs_s)
        bs_s[...] = jnp.zeros_like(bs_s)
        cs_s[...] = jnp.zeros_like(cs_s)
        dt_s[...] = jnp.zeros_like(dt_s)

    def n_tile():
        return n_ref[...].reshape(rows, D_MODEL)

    _store_tile(xpad, pmc, _dot(n_tile(), wxbc_ref[...]), nb, tt)
    dt_all = _softplus(_dot(n_tile(), wdt_ref[...]) + dtb_ref[...])
    if seq_mode:
        dt_s[...] = dt_all
    else:
        for b in range(nb):
            dt_s[b * L:b * L + tt, :] = dt_all[b * tt:(b + 1) * tt, :]

    def conv(i):
        b, r = _chunk_pos(i, cpb, rc)
        dst = i * (rc if seq_mode else L)
        for lb, ls in enumerate(_lane_blocks(SSM_CONV_DIM)):
            acc = None
            for k in range(SSM_CONV):
                term = xpad[b, lb, pl.ds(pmc - (SSM_CONV - 1) + k + r, rc), :] * cw_ref[k:k + 1, ls]
                acc = term if acc is None else acc + term
            xbc = _silu(acc + cb_ref[:, ls])
            if ls.start < SSM_INNER:
                xs_s[pl.ds(dst, rc), ls] = xbc
            elif ls.start < SSM_INNER + SSM_GN:
                bs_s[pl.ds(dst, rc), ls.start - SSM_INNER:ls.stop - SSM_INNER] = xbc
            else:
                cs_s[pl.ds(dst, rc), ls.start - SSM_INNER - SSM_GN:ls.stop - SSM_INNER - SSM_GN] = xbc

    _interleave(rows // rc, conv, _slabs(zs, n_tile, wz_ref))
    _carry_conv_tail(xpad, pmc, SSM_CONV, tt, nmc_ref, is_last)

    a_row = -jnp.exp(alog_ref[...])
    ii = lax.broadcasted_iota(jnp.int32, (L, L), 0)
    jj = lax.broadcasted_iota(jnp.int32, (L, L), 1)
    causal = ii >= jj
    tril = jnp.where(causal, 1.0, 0.0).astype(BF16)
    low_half = (lax.broadcasted_iota(jnp.int32, (L, GROUP_W), 1) % LANES) < SSM_HEAD_DIM

    def pack_terms(x):
        head_lane = lax.broadcasted_iota(jnp.int32, x.shape, 1) < SSM_HEADS
        r = jnp.where(head_lane, x, 0.0)
        packed = None
        for k in range(SPLIT_TERMS):
            p = r.astype(BF16).astype(F32)
            placed = p if k == 0 else pltpu.roll(p, k * SSM_HEADS, 1)
            packed = placed if packed is None else packed + placed
            r = r - p
        return packed.astype(BF16)

    def expand(x):
        return _dot(pack_terms(x), e_ref[...])

    def chunk(c):
        if not seq_mode:
            load_state(c)
        r0 = c * L
        rv = c * lv
        dtc = dt_s[pl.ds(r0, L), :]
        cs3 = _dot(tril, pack_terms(dtc * a_row))
        acs = cs3
        for k in range(1, SPLIT_TERMS):
            acs = acs + pltpu.roll(cs3, LANES - k * SSM_HEADS, 1)
        acs_s[...] = acs
        acst_s[...] = acs.T
        a_last = acs[L - 1:L, :]
        dt_e = expand(dtc)
        eacs_e = expand(jnp.exp(acs))
        dec_e = expand(jnp.exp(a_last - acs))
        elast_e = expand(jnp.broadcast_to(jnp.exp(a_last), (SUBLANES, LANES)))[0:1, :]
        for g in range(SSM_GROUPS):
            gs = slice(g * GROUP_W, (g + 1) * GROUP_W)
            ns = slice(g * SSM_STATE, (g + 1) * SSM_STATE)
            xs_g = xs_s[pl.ds(r0, L), gs]
            xdt = xs_g * dt_e[:, gs]
            x_even = jnp.where(low_half, xdt, 0.0).astype(BF16)
            x_odd = jnp.where(low_half, 0.0, xdt).astype(BF16)
            xdw_b = (xdt * dec_e[:, gs]).astype(BF16)
            c_g = cs_s[pl.ds(r0, L), ns].astype(BF16)
            bt_g = bs_s[pl.ds(r0, L), ns].T.astype(BF16)
            cb = _dot(c_g, bt_g)
            ht_g = ht[:, gs]
            y_off = _dot(c_g, ht_g.astype(BF16)) * eacs_e[:, gs]
            ht[:, gs] = ht_g * elast_e[:, gs] + _dot(bt_g, xdw_b)
            y_cols = []
            for pr in range(GROUP_W // LANES):
                hd = g * (GROUP_W // SSM_HEAD_DIM) + 2 * pr
                ps = slice(pr * LANES, (pr + 1) * LANES)
                ms = []
                for hh in (hd, hd + 1):
                    diff = acs_s[:, hh:hh + 1] - acst_s[hh:hh + 1, :]
                    ms.append((cb * jnp.exp(jnp.where(causal, diff, -jnp.inf))).astype(BF16))
                y_cols.append(_dot(jnp.concatenate(ms, axis=1),
                                   jnp.concatenate([x_even[:, ps], x_odd[:, ps]], axis=0)))
            y = jnp.concatenate(y_cols, axis=-1) + y_off + de_ref[:, gs] * xs_g
            y = y[0:lv, :] * _silu(zs[pl.ds(rv, lv), gs])
            y = y * lax.rsqrt(jnp.mean(y * y, axis=-1, keepdims=True) + EPS)
            ys[pl.ds(rv, lv), gs] = (y * ng_ref[:, gs]).astype(BF16)
        if not seq_mode:
            store_state(c)

    def finish(b, r, m):
        y_b = _dot(ys[pl.ds(b * tt + r, m), :], wso_ref[...])
        merged = mac_ref[b, pl.ds(r, m), :] + gb_ref[b, pl.ds(r, m), :] * y_b
        h1_ref[b, pl.ds(r, m), :] = h_ref[b, pl.ds(r, m), :] + _dot(merged.astype(BF16), wo_ref[...])

    for c in range(n_chunks):
        chunk(c)
        if seq_mode:
            finish(0, c * L, L)

    if seq_mode:
        @pl.when(is_last)
        def _():
            store_state(0)
    else:
        y_b = _dot(ys[...], wso_ref[...])
        merged = mac_ref[...].reshape(rows, D_MODEL) + gb_ref[...].reshape(rows, D_MODEL) * y_b
        h1 = h_ref[...].reshape(rows, D_MODEL) + _dot(merged.astype(BF16), wo_ref[...])
        h1_ref[...] = h1.reshape(nb, tt, D_MODEL)


def _ffn_kernel(h_ref, p_ref, stff_ref, gffn_ref, wup_ref, fw_ref, fb_ref, wdn_ref, gple_ref,
                wpg_ref, wpp_ref, gfin_ref,
                out_ref, nff_ref,
                xpad, nf_s, lhs, *, nb, tt, rc, final):
    t = pl.program_id(1)
    is_last = t == pl.num_programs(1) - 1
    rows = nb * tt
    cpb = tt // rc
    pff = _pad_rows(FF_KERNEL)
    half_blocks = FF_DIM // LANES
    slab_blocks = MXU_DIM // LANES
    n_slabs = FF_DIM // MXU_DIM

    @pl.when(t == 0)
    def _():
        _load_conv_tail(xpad, pff, FF_KERNEL, stff_ref)

    h1 = h_ref[...].reshape(rows, D_MODEL)
    nf_s[...] = _rms(h1, gffn_ref[...]).astype(BF16)

    def up_slab(j):
        for half in range(2):
            lo = half * FF_DIM + j * MXU_DIM
            _store_tile(xpad, pff, _dot(nf_s[...], wup_ref[:, lo:lo + MXU_DIM]), nb, tt, first_block=lo // LANES)

    def conv_slab(j):
        for i in range(rows // rc):
            b, r = _chunk_pos(i, cpb, rc)
            r2 = i * rc
            for lb in range(j * slab_blocks, (j + 1) * slab_blocks):
                halves = []
                for blk in (lb, half_blocks + lb):
                    cs = slice(blk * LANES, (blk + 1) * LANES)
                    acc = None
                    for k in range(FF_KERNEL):
                        term = xpad[b, blk, pl.ds(pff - (FF_KERNEL - 1) + k + r, rc), :] * fw_ref[k:k + 1, cs]
                        acc = term if acc is None else acc + term
                    halves.append(acc + fb_ref[:, cs])
                lhs[pl.ds(r2, rc), lb * LANES:(lb + 1) * LANES] = (_silu(halves[0]) * halves[1]).astype(BF16)

    up_slab(0)
    for j in range(n_slabs):
        if j + 1 < n_slabs:
            up_slab(j + 1)
        conv_slab(j)
    _carry_conv_tail(xpad, pff, FF_KERNEL, tt, nff_ref, is_last)

    h2 = h1 + _dot(lhs[...], wdn_ref[...])
    gate = _sigmoid(_dot(_rms(h2, gple_ref[...]).astype(BF16), wpg_ref[...]))
    pp = _dot(p_ref[...].reshape(rows, PLE_DIM).astype(BF16), wpp_ref[...])
    h3 = h2 + pp * gate
    if final:
        h3 = _rms(h3, gfin_ref[...])
    out_ref[...] = h3.reshape(nb, tt, D_MODEL)


def _tile_spec(nb, tt, c):
    return pl.BlockSpec((nb, tt, c), lambda b, t: (b, t, 0))


def _state_spec(nb, r, c):
    return pl.BlockSpec((nb, r, c), lambda b, t: (b, 0, 0))


def _const_spec(arr):
    return pl.BlockSpec(arr.shape, lambda b, t: (0,) * arr.ndim, pipeline_mode=pl.Buffered(1))


def _params():
    return pltpu.CompilerParams(dimension_semantics=("arbitrary", "arbitrary"),
                                vmem_limit_bytes=VMEM_LIMIT)


def _conv_buffer(nb, k, tt, c):
    return pltpu.VMEM((nb, c // LANES, _pad_rows(k) + tt, LANES), F32)


def _ac_call(h, st_sc, st_cf, w, nb, tt):
    bsz, tlen, _ = h.shape
    rc = min(tt, 32)
    rows = nb * tt
    consts = [w['g_mix'], w['w_ac'], w['sc_conv_w'], w['w_sc_out'],
              w['cf_conv_w'], w['cf_conv_b'], w['cf_ln_g'], w['cf_ln_b'], w['w_cf_out']]
    return pl.pallas_call(
        functools.partial(_ac_kernel, nb=nb, tt=tt, rc=rc),
        grid=(bsz // nb, tlen // tt),
        in_specs=[_tile_spec(nb, tt, D_MODEL), _state_spec(nb, SC_KERNEL - 1, D_MODEL),
                  _state_spec(nb, CF_KERNEL - 1, D_MODEL)] + [_const_spec(a) for a in consts],
        out_specs=[_tile_spec(nb, tt, D_MODEL), _tile_spec(nb, tt, D_MODEL), _tile_spec(nb, tt, D_MODEL),
                   _state_spec(nb, SC_KERNEL - 1, D_MODEL), _state_spec(nb, CF_KERNEL - 1, D_MODEL)],
        out_shape=[jax.ShapeDtypeStruct((bsz, tlen, D_MODEL), BF16),
                   jax.ShapeDtypeStruct((bsz, tlen, D_MODEL), F32),
                   jax.ShapeDtypeStruct((bsz, tlen, D_MODEL), F32),
                   jax.ShapeDtypeStruct((bsz, SC_KERNEL - 1, D_MODEL), F32),
                   jax.ShapeDtypeStruct((bsz, CF_KERNEL - 1, D_MODEL), F32)],
        scratch_shapes=[_conv_buffer(nb, SC_KERNEL, tt, D_MODEL),
                        _conv_buffer(nb, CF_KERNEL, tt, D_MODEL),
                        pltpu.VMEM((rows, OFF_Z + 2 * D_MODEL), F32),
                        pltpu.VMEM((rows, D_MODEL), F32),
                        pltpu.VMEM((rows, D_MODEL), BF16),
                        pltpu.VMEM((rows, D_MODEL), BF16),
                        pltpu.VMEM((rows, D_MODEL), F32),
                        pltpu.VMEM((rows, D_MODEL), F32)],
        compiler_params=_params(),
        name="mix_ac",
    )(h, st_sc, st_cf, *consts)


def _ssm_call(n, h, mac, gb, st_mc, st_ssm, w, nb, tt):
    bsz, tlen, _ = h.shape
    rc = min(tt, 32)
    rows = nb * tt
    L = SSD_CHUNK
    prow = rows if tt >= L else nb * L
    consts = [w['w_z'], w['w_xbc'], w['w_dt'], w['ssm_conv_w'], w['ssm_conv_b'], w['dt_bias'], w['a_log'],
              w['d_exp'], w['ssm_norm_g'], w['w_ssm_out'], w['w_o'], w['head_expand']]
    st_ssm2 = st_ssm.reshape(bsz, SSM_INNER, SSM_STATE)
    h1, nmc, nssm = pl.pallas_call(
        functools.partial(_ssm_kernel, nb=nb, tt=tt, rc=rc),
        grid=(bsz // nb, tlen // tt),
        in_specs=[_tile_spec(nb, tt, D_MODEL)] * 4
        + [_state_spec(nb, SSM_CONV - 1, SSM_CONV_DIM), _state_spec(nb, SSM_INNER, SSM_STATE)]
        + [_const_spec(a) for a in consts],
        out_specs=[_tile_spec(nb, tt, D_MODEL), _state_spec(nb, SSM_CONV - 1, SSM_CONV_DIM),
                   _state_spec(nb, SSM_INNER, SSM_STATE)],
        out_shape=[jax.ShapeDtypeStruct((bsz, tlen, D_MODEL), F32),
                   jax.ShapeDtypeStruct((bsz, SSM_CONV - 1, SSM_CONV_DIM), F32),
                   jax.ShapeDtypeStruct((bsz, SSM_INNER, SSM_STATE), F32)],
        scratch_shapes=[_conv_buffer(nb, SSM_CONV, tt, SSM_CONV_DIM),
                        pltpu.VMEM((rows, SSM_INNER), F32),
                        pltpu.VMEM((prow, SSM_INNER), F32),
                        pltpu.VMEM((prow, SSM_GN), F32),
                        pltpu.VMEM((prow, SSM_GN), F32),
                        pltpu.VMEM((prow, LANES), F32),
                        pltpu.VMEM((L, LANES), F32),
                        pltpu.VMEM((LANES, L), F32),
                        pltpu.VMEM((SSM_STATE, SSM_INNER), F32),
                        pltpu.VMEM((rows, SSM_INNER), BF16)],
        compiler_params=_params(),
        name="mix_ssm",
    )(n, h, mac, gb, st_mc, st_ssm2, *consts)
    return h1, nmc, nssm.reshape(bsz, SSM_HEADS, SSM_HEAD_DIM, SSM_STATE)


def _ffn_call(h1, p, st_ff, w, g_final, nb, tt, final):
    bsz, tlen, _ = h1.shape
    rc = min(tt, 32)
    rows = nb * tt
    consts = [w['g_ffn'], w['w_up'], w['ff_conv_w'], w['ff_conv_b'], w['w_down'], w['g_ple'],
              w['w_ple_gate'], w['w_ple_proj'], g_final]
    return pl.pallas_call(
        functools.partial(_ffn_kernel, nb=nb, tt=tt, rc=rc, final=final),
        grid=(bsz // nb, tlen // tt),
        in_specs=[_tile_spec(nb, tt, D_MODEL), _tile_spec(nb, tt, PLE_DIM),
                  _state_spec(nb, FF_KERNEL - 1, 2 * FF_DIM)] + [_const_spec(a) for a in consts],
        out_specs=[_tile_spec(nb, tt, D_MODEL), _state_spec(nb, FF_KERNEL - 1, 2 * FF_DIM)],
        out_shape=[jax.ShapeDtypeStruct((bsz, tlen, D_MODEL), F32),
                   jax.ShapeDtypeStruct((bsz, FF_KERNEL - 1, 2 * FF_DIM), F32)],
        scratch_shapes=[_conv_buffer(nb, FF_KERNEL, tt, 2 * FF_DIM),
                        pltpu.VMEM((rows, D_MODEL), BF16),
                        pltpu.VMEM((rows, FF_DIM), BF16)],
        compiler_params=_params(),
        name="ffn_ple",
    )(h1, p, st_ff, *consts)


def _ac_columns(wi):
    groups = []
    for lb in range(D_MODEL // LANES):
        groups += [wi[:, OFF_CF + lb * LANES:OFF_CF + (lb + 1) * LANES],
                   wi[:, OFF_CF + D_MODEL + lb * LANES:OFF_CF + D_MODEL + (lb + 1) * LANES],
                   wi[:, lb * AC_GROUP_AG:(lb + 1) * AC_GROUP_AG]]
    return jnp.concatenate(groups, axis=1)


def _prep_layer(i, g_mix, w_in, sc_conv_w, w_sc_out, ssm_conv_w, ssm_conv_b, ssm_dt_bias, ssm_a_log, ssm_d,
                ssm_norm_g, w_ssm_out, cf_conv_w, cf_conv_b, cf_ln_g, cf_ln_b, w_cf_out, w_o, g_ffn, w_up,
                ff_conv_w, ff_conv_b, w_down, g_ple, w_ple_gate, w_ple_proj):
    wi = w_in[i]
    row = lambda v: v[i].reshape(1, -1)
    lane_pad = lambda v: jnp.pad(v, [(0, 0)] * (v.ndim - 1) + [(0, LANES - v.shape[-1])])
    head_of_channel = jnp.arange(SSM_INNER) // SSM_HEAD_DIM
    term_lane = jnp.arange(LANES)
    return {
        'g_mix': row(g_mix),
        'w_ac': _ac_columns(wi).astype(BF16),
        'w_z': wi[:, OFF_Z:OFF_XBC].astype(BF16),
        'w_xbc': wi[:, OFF_XBC:OFF_DT].astype(BF16),
        'w_dt': lane_pad(wi[:, OFF_DT:OFF_CF]).astype(BF16),
        'sc_conv_w': sc_conv_w[i],
        'w_sc_out': w_sc_out[i].astype(BF16),
        'ssm_conv_w': ssm_conv_w[i],
        'ssm_conv_b': row(ssm_conv_b),
        'dt_bias': lane_pad(row(ssm_dt_bias)),
        'a_log': lane_pad(row(ssm_a_log)),
        'd_exp': jnp.repeat(ssm_d[i], SSM_HEAD_DIM).reshape(1, SSM_INNER),
        'ssm_norm_g': row(ssm_norm_g),
        'w_ssm_out': w_ssm_out[i].astype(BF16),
        'cf_conv_w': cf_conv_w[i],
        'cf_conv_b': row(cf_conv_b),
        'cf_ln_g': row(cf_ln_g),
        'cf_ln_b': row(cf_ln_b),
        'w_cf_out': w_cf_out[i].astype(BF16),
        'w_o': w_o[i].astype(BF16),
        'g_ffn': row(g_ffn),
        'w_up': w_up[i].astype(BF16),
        'ff_conv_w': ff_conv_w[i],
        'ff_conv_b': row(ff_conv_b),
        'w_down': w_down[i].astype(BF16),
        'g_ple': row(g_ple),
        'w_ple_gate': w_ple_gate[i].astype(BF16),
        'w_ple_proj': w_ple_proj[i].astype(BF16),
        'head_expand': ((term_lane[:, None] % SSM_HEADS == head_of_channel[None, :])
                        & (term_lane[:, None] < SPLIT_TERMS * SSM_HEADS)).astype(BF16),
    }


def _run_trunk(x, p, st_sc, st_mc, st_ssm, st_cf, st_ff, layers, g_final, tiles):
    h = x
    outs = [[], [], [], [], []]
    for i, w in enumerate(layers):
        n, mac, gb, nsc, ncf = _ac_call(h, st_sc[i], st_cf[i], w, *tiles['ac'])
        h1, nmc, nssm = _ssm_call(n, h, mac, gb, st_mc[i], st_ssm[i], w, *tiles['ssm'])
        h, nff = _ffn_call(h1, p[i], st_ff[i], w, g_final, *tiles['ffn'], final=(i == len(layers) - 1))
        for lst, s in zip(outs, (nsc, nmc, nssm, ncf, nff)):
            lst.append(s)
    return h, [jnp.stack(lst) for lst in outs]


def _tiles_for(bsz, tlen):
    if tlen >= 256:
        return {'ac': (1, 256), 'ssm': (1, 256), 'ffn': (1, 256)}
    return {'ac': (min(bsz, 16), tlen), 'ssm': (min(bsz, 4), tlen), 'ffn': (min(bsz, 16), tlen)}


def kernel(x_prompt, x_sample, p_prompt, p_sample, state_short_conv, state_ssm_conv, state_ssm, state_cf_conv, state_ffn_conv, g_mix, w_in, sc_conv_w, w_sc_out, ssm_conv_w, ssm_conv_b, ssm_dt_bias, ssm_a_log, ssm_d, ssm_norm_g, w_ssm_out, cf_conv_w, cf_conv_b, cf_ln_g, cf_ln_b, w_cf_out, w_o, g_ffn, w_up, ff_conv_w, ff_conv_b, w_down, g_ple, w_ple_gate, w_ple_proj, g_final):
    layers = [_prep_layer(i, g_mix, w_in, sc_conv_w, w_sc_out, ssm_conv_w, ssm_conv_b, ssm_dt_bias, ssm_a_log,
                          ssm_d, ssm_norm_g, w_ssm_out, cf_conv_w, cf_conv_b, cf_ln_g, cf_ln_b, w_cf_out, w_o,
                          g_ffn, w_up, ff_conv_w, ff_conv_b, w_down, g_ple, w_ple_gate, w_ple_proj)
              for i in range(DEPTH)]
    gfin = g_final.reshape(1, D_MODEL)
    bp, tp, _ = x_prompt.shape
    bs, ts, _ = x_sample.shape
    z_sc = jnp.zeros((DEPTH, bp, SC_KERNEL - 1, D_MODEL), F32)
    z_mc = jnp.zeros((DEPTH, bp, SSM_CONV - 1, SSM_CONV_DIM), F32)
    z_ssm = jnp.zeros((DEPTH, bp, SSM_HEADS, SSM_HEAD_DIM, SSM_STATE), F32)
    z_cf = jnp.zeros((DEPTH, bp, CF_KERNEL - 1, D_MODEL), F32)
    z_ff = jnp.zeros((DEPTH, bp, FF_KERNEL - 1, 2 * FF_DIM), F32)
    y_p, sp = _run_trunk(x_prompt, p_prompt, z_sc, z_mc, z_ssm, z_cf, z_ff, layers, gfin, _tiles_for(bp, tp))
    y_s, ss = _run_trunk(x_sample, p_sample, state_short_conv, state_ssm_conv, state_ssm, state_cf_conv,
                         state_ffn_conv, layers, gfin, _tiles_for(bs, ts))
    return (y_p, y_s, sp[0], sp[1], sp[2], sp[3], sp[4], ss[0], ss[1], ss[2], ss[3], ss[4])
```

```python
import functools

import jax
import jax.numpy as jnp
from jax import lax
from jax.experimental import pallas as pl
from jax.experimental.pallas import tpu as pltpu

F32 = jnp.float32
BF16 = jnp.bfloat16

D_MODEL = 1024
DEPTH = 2
PLE_DIM = 256
EPS = 1e-6
SC_KERNEL = 3
SSM_INNER = 2 * D_MODEL
SSM_HEAD_DIM = 64
SSM_HEADS = SSM_INNER // SSM_HEAD_DIM
SSM_GROUPS = 4
SSM_STATE = 128
SSM_CONV = 4
SSM_GN = SSM_GROUPS * SSM_STATE
SSM_CONV_DIM = SSM_INNER + 2 * SSM_GN
CF_KERNEL = 31
FF_DIM = 2816
FF_KERNEL = 3
OFF_SC = 3 * D_MODEL
OFF_Z = OFF_SC + 3 * D_MODEL
OFF_XBC = OFF_Z + SSM_INNER
OFF_DT = OFF_XBC + SSM_CONV_DIM
OFF_CF = OFF_DT + SSM_HEADS
N_IN = OFF_CF + 2 * D_MODEL

LANES = 128
SUBLANES = 8
MXU_DIM = 256
SSD_CHUNK = 128
GROUP_W = SSM_INNER // SSM_GROUPS
SPLIT_TERMS = 3
SLAB = 2 * MXU_DIM
AC_STEPS = 4
VMEM_LIMIT = 56 * 1024 * 1024


def _dot(a, b):
    return jnp.dot(a, b, preferred_element_type=F32)


def _sigmoid(x):
    return 1.0 / (1.0 + jnp.exp(-x))


def _silu(x):
    return x * _sigmoid(x)


def _softplus(x):
    return jnp.maximum(x, 0.0) + jnp.log1p(jnp.exp(-jnp.abs(x)))


def _rms(x, g):
    return x * lax.rsqrt(jnp.mean(x * x, axis=-1, keepdims=True) + EPS) * g


def _pad_rows(k):
    return -(-(k - 1) // SUBLANES) * SUBLANES


def _chunk_pos(i, cpb, rc):
    return i // cpb, (i % cpb) * rc


def _for_each(n, body):
    for i in range(n):
        body(i)


def _slabs(dst, lhs_fn, w_ref):
    thunks = []
    for lo in range(0, w_ref.shape[1], SLAB):
        cs = slice(lo, min(lo + SLAB, w_ref.shape[1]))

        def run(cs=cs):
            dst[:, cs] = _dot(lhs_fn(), w_ref[:, cs])

        thunks.append(run)
    return thunks


def _interleave(n_steps, step, thunks):
    done = 0
    for i in range(n_steps):
        upto = (len(thunks) * (i + 1)) // n_steps
        for th in thunks[done:upto]:
            th()
        done = upto
        step(i)


def _lane_blocks(c):
    return [slice(lb * LANES, (lb + 1) * LANES) for lb in range(c // LANES)]


def _load_conv_tail(xpad, pad, k, st_ref):
    for lb, ls in enumerate(_lane_blocks(st_ref.shape[-1])):
        xpad[:, lb, pad - (k - 1):pad, :] = st_ref[:, :, ls]


def _carry_conv_tail(xpad, pad, k, tt, new_ref, is_last):
    tail = xpad[:, :, pad + tt - (k - 1):pad + tt, :]

    @pl.when(is_last)
    def _():
        for lb, ls in enumerate(_lane_blocks(new_ref.shape[-1])):
            new_ref[:, :, ls] = tail[:, lb]

    xpad[:, :, pad - (k - 1):pad, :] = tail


def _store_tile(xpad, pad, val, nb, tt, first_block=0):
    for lb, ls in enumerate(_lane_blocks(val.shape[-1])):
        xpad[:, first_block + lb, pad:pad + tt, :] = val[:, ls].reshape(nb, tt, LANES)


def _ac_kernel(h_ref, stsc_ref, stcf_ref, gmix_ref, wag_ref, wcf_ref, scw_ref, wsco_ref,
               cfw_ref, cfb_ref, lng_ref, lnb_ref, wcfo_ref,
               n_ref, mac_ref, gb_ref, nsc_ref, ncf_ref,
               xsc, xcf, proj_ag, proj_c, conv_c, lhs_a, lhs_c, ya, yc, *, nb, tt, rc):
    t = pl.program_id(1)
    is_last = t == pl.num_programs(1) - 1
    rows = nb * tt
    cpb = tt // rc
    chunks = rows // rc
    psc, pcf = _pad_rows(SC_KERNEL), _pad_rows(CF_KERNEL)
    n_slabs = wag_ref.shape[0]
    steps = AC_STEPS
    cps, sps = chunks // steps, n_slabs // steps

    @pl.when(t == 0)
    def _():
        _load_conv_tail(xsc, psc, SC_KERNEL, stsc_ref)
        _load_conv_tail(xcf, pcf, CF_KERNEL, stcf_ref)

    h = h_ref[...].reshape(rows, D_MODEL)
    n_ref[...] = _rms(h, gmix_ref[...]).astype(BF16).reshape(nb, tt, D_MODEL)

    def n_tile():
        return n_ref[...].reshape(rows, D_MODEL)

    def ag(rs, col, width=LANES):
        return proj_ag[col // SLAB, rs, col % SLAB:col % SLAB + width]

    proj_c[...] = _dot(n_tile(), wcf_ref[...])

    def c_fill(i):
        b, r = _chunk_pos(i, cpb, rc)
        r2 = i * rc
        for lb, ls in enumerate(_lane_blocks(D_MODEL)):
            c_g = proj_c[pl.ds(r2, rc), D_MODEL + lb * LANES:D_MODEL + (lb + 1) * LANES]
            xcf[b, lb, pl.ds(pcf + r, rc), :] = proj_c[pl.ds(r2, rc), ls] * _sigmoid(c_g)

    _for_each(chunks, c_fill)

    def c_step(s, carry):
        for q in range(sps):
            proj_ag[s * sps + q] = _dot(n_tile(), wag_ref[s * sps + q])
        for lb, ls in enumerate(_lane_blocks(D_MODEL)):
            taps = [jnp.broadcast_to(cfw_ref[k:k + 1, ls], (SUBLANES, LANES)) for k in range(CF_KERNEL)]
            for q in range(cps):
                i = s * cps + q
                b, r = _chunk_pos(i, cpb, rc)
                for rg in range(0, rc, SUBLANES):
                    acc = None
                    for k in range(CF_KERNEL):
                        term = xcf[b, lb, pl.ds(pcf - (CF_KERNEL - 1) + k + r + rg, SUBLANES), :] * taps[k]
                        acc = term if acc is None else acc + term
                    conv_c[pl.ds(pl.multiple_of(i * rc + rg, SUBLANES), SUBLANES), ls] = acc
        for q in range(cps):
            r2 = pl.multiple_of((s * cps + q) * rc, rc)
            v = conv_c[pl.ds(r2, rc), :] + cfb_ref[...]
            mu = jnp.mean(v, axis=-1, keepdims=True)
            vc = v - mu
            var = jnp.mean(vc * vc, axis=-1, keepdims=True)
            ln = vc * lax.rsqrt(var + EPS) * lng_ref[...] + lnb_ref[...]
            lhs_c[pl.ds(r2, rc), :] = _silu(ln).astype(BF16)
        return carry

    lax.fori_loop(0, steps, c_step, 0)
    _carry_conv_tail(xcf, pcf, CF_KERNEL, tt, ncf_ref, is_last)

    def a_fill(i):
        b, r = _chunk_pos(i, cpb, rc)
        rs = pl.ds(i * rc, rc)
        for lb, ls in enumerate(_lane_blocks(D_MODEL)):
            xsc[b, lb, pl.ds(psc + r, rc), :] = (ag(rs, OFF_SC + D_MODEL + lb * LANES)
                                                 * ag(rs, OFF_SC + 2 * D_MODEL + lb * LANES))

    _for_each(chunks, a_fill)

    def a_conv(i):
        b, r = _chunk_pos(i, cpb, rc)
        rs = pl.ds(i * rc, rc)
        for lb, ls in enumerate(_lane_blocks(D_MODEL)):
            u = None
            for k in range(SC_KERNEL):
                term = xsc[b, lb, pl.ds(psc - (SC_KERNEL - 1) + k + r, rc), :] * scw_ref[k:k + 1, ls]
                u = term if u is None else u + term
            lhs_a[rs, ls] = (ag(rs, OFF_SC + lb * LANES) * u).astype(BF16)

    _for_each(chunks, a_conv)
    _carry_conv_tail(xsc, psc, SC_KERNEL, tt, nsc_ref, is_last)
    yc[...] = _dot(lhs_c[...], wcfo_ref[...])
    ya[...] = _dot(lhs_a[...], wsco_ref[...])

    def g_merge(i):
        b, r = _chunk_pos(i, cpb, rc)
        rs = pl.ds(i * rc, rc)
        for lo in range(0, D_MODEL, SLAB):
            cs = slice(lo, lo + SLAB)
            g_a = _sigmoid(ag(rs, lo, SLAB))
            g_b = _sigmoid(ag(rs, D_MODEL + lo, SLAB))
            g_c = _sigmoid(ag(rs, 2 * D_MODEL + lo, SLAB))
            mac_ref[b, pl.ds(r, rc), cs] = g_a * ya[rs, cs] + g_c * yc[rs, cs]
            gb_ref[b, pl.ds(r, rc), cs] = g_b

    _for_each(chunks, g_merge)


def _ssm_kernel(n_ref, h_ref, mac_ref, gb_ref, stmc_ref, stssm_ref,
                wz_ref, wxbc_ref, wdt_ref, cw_ref, cb_ref, dtb_ref, alog_ref, de_ref, ng_ref,
                wso_ref, wo_ref, e_ref, *rest, nb, tt, rc):
    h1_ref, nmc_ref, nssm_ref, xpad, zs, xs_s, bs_s, cs_s, dt_s, acs_s, acst_s, ht, ys = rest[-13:]
    t = pl.program_id(1)
    is_last = t == pl.num_programs(1) - 1
    rows = nb * tt
    cpb = tt // rc
    pmc = _pad_rows(SSM_CONV)
    L = SSD_CHUNK
    seq_mode = tt >= L
    lv = L if seq_mode else tt
    n_chunks = rows // lv

    @pl.when(t == 0)
    def _():
        _load_conv_tail(xpad, pmc, SSM_CONV, stmc_ref)

    def load_state(b):
        for j in range(SSM_INNER // LANES):
            ht[:, j * LANES:(j + 1) * LANES] = stssm_ref[b, j * LANES:(j + 1) * LANES, :].T

    def store_state(b):
        for j in range(SSM_INNER // LANES):
            nssm_ref[b, j * LANES:(j + 1) * LANES, :] = ht[:, j * LANES:(j + 1) * LANES].T

    if seq_mode:
        @pl.when(t == 0)
        def _():
            load_state(0)
    else:
        xs_s[...] = jnp.zeros_like(xs_s)
        bs_s[...] = jnp.zeros_like(bs_s)
        cs_s[...] = jnp.zeros_like(cs_s)
        dt_s[...] = jnp.zeros_like(dt_s)

    def n_tile():
        return n_ref[...].reshape(rows, D_MODEL)

    _store_tile(xpad, pmc, _dot(n_tile(), wxbc_ref[...]), nb, tt)
    dt_all = _softplus(_dot(n_tile(), wdt_ref[...]) + dtb_ref[...])
    if seq_mode:
        dt_s[...] = dt_all
    else:
        for b in range(nb):
            dt_s[b * L:b * L + tt, :] = dt_all[b * tt:(b + 1) * tt, :]

    def conv(i):
        b, r = _chunk_pos(i, cpb, rc)
        dst = i * (rc if seq_mode else L)
        for lb, ls in enumerate(_lane_blocks(SSM_CONV_DIM)):
            acc = None
            for k in range(SSM_CONV):
                term = xpad[b, lb, pl.ds(pmc - (SSM_CONV - 1) + k + r, rc), :] * cw_ref[k:k + 1, ls]
                acc = term if acc is None else acc + term
            xbc = _silu(acc + cb_ref[:, ls])
            if ls.start < SSM_INNER:
                xs_s[pl.ds(dst, rc), ls] = xbc
            elif ls.start < SSM_INNER + SSM_GN:
                bs_s[pl.ds(dst, rc), ls.start - SSM_INNER:ls.stop - SSM_INNER] = xbc
            else:
                cs_s[pl.ds(dst, rc), ls.start - SSM_INNER - SSM_GN:ls.stop - SSM_INNER - SSM_GN] = xbc

    _interleave(rows // rc, conv, _slabs(zs, n_tile, wz_ref))
    _carry_conv_tail(xpad, pmc, SSM_CONV, tt, nmc_ref, is_last)

    a_row = -jnp.exp(alog_ref[...])
    ii = lax.broadcasted_iota(jnp.int32, (L, L), 0)
    jj = lax.broadcasted_iota(jnp.int32, (L, L), 1)
    causal = ii >= jj
    tril = jnp.where(causal, 1.0, 0.0).astype(BF16)
    low_half = (lax.broadcasted_iota(jnp.int32, (L, GROUP_W), 1) % LANES) < SSM_HEAD_DIM

    def pack_terms(x):
        head_lane = lax.broadcasted_iota(jnp.int32, x.shape, 1) < SSM_HEADS
        r = jnp.where(head_lane, x, 0.0)
        packed = None
        for k in range(SPLIT_TERMS):
            p = r.astype(BF16).astype(F32)
            placed = p if k == 0 else pltpu.roll(p, k * SSM_HEADS, 1)
            packed = placed if packed is None else packed + placed
            r = r - p
        return packed.astype(BF16)

    def expand(x):
        return _dot(pack_terms(x), e_ref[...])

    def chunk(c):
        if not seq_mode:
            load_state(c)
        r0 = c * L
        rv = c * lv
        dtc = dt_s[pl.ds(r0, L), :]
        cs3 = _dot(tril, pack_terms(dtc * a_row))
        acs = cs3
        for k in range(1, SPLIT_TERMS):
            acs = acs + pltpu.roll(cs3, LANES - k * SSM_HEADS, 1)
        acs_s[...] = acs
        acst_s[...] = acs.T
        a_last = acs[L - 1:L, :]
        dt_e = expand(dtc)
        eacs_e = expand(jnp.exp(acs))
        dec_e = expand(jnp.exp(a_last - acs))
        elast_e = expand(jnp.broadcast_to(jnp.exp(a_last), (SUBLANES, LANES)))[0:1, :]
        for g in range(SSM_GROUPS):
            gs = slice(g * GROUP_W, (g + 1) * GROUP_W)
            ns = slice(g * SSM_STATE, (g + 1) * SSM_STATE)
            xs_g = xs_s[pl.ds(r0, L), gs]
            xdt = xs_g * dt_e[:, gs]
            x_even = jnp.where(low_half, xdt, 0.0).astype(BF16)
            x_odd = jnp.where(low_half, 0.0, xdt).astype(BF16)
            xdw_b = (xdt * dec_e[:, gs]).astype(BF16)
            c_g = cs_s[pl.ds(r0, L), ns].astype(BF16)
            bt_g = bs_s[pl.ds(r0, L), ns].T.astype(BF16)
            cb = _dot(c_g, bt_g)
            ht_g = ht[:, gs]
            y_off = _dot(c_g, ht_g.astype(BF16)) * eacs_e[:, gs]
            ht[:, gs] = ht_g * elast_e[:, gs] + _dot(bt_g, xdw_b)
            y_cols = []
            for pr in range(GROUP_W // LANES):
                hd = g * (GROUP_W // SSM_HEAD_DIM) + 2 * pr
                ps = slice(pr * LANES, (pr + 1) * LANES)
                ms = []
                for hh in (hd, hd + 1):
                    diff = acs_s[:, hh:hh + 1] - acst_s[hh:hh + 1, :]
                    ms.append((cb * jnp.exp(jnp.where(causal, diff, -jnp.inf))).astype(BF16))
                y_cols.append(_dot(jnp.concatenate(ms, axis=1),
                                   jnp.concatenate([x_even[:, ps], x_odd[:, ps]], axis=0)))
            y = jnp.concatenate(y_cols, axis=-1) + y_off + de_ref[:, gs] * xs_g
            y = y[0:lv, :] * _silu(zs[pl.ds(rv, lv), gs])
            y = y * lax.rsqrt(jnp.mean(y * y, axis=-1, keepdims=True) + EPS)
            ys[pl.ds(rv, lv), gs] = (y * ng_ref[:, gs]).astype(BF16)
        if not seq_mode:
            store_state(c)

    def finish(b, r, m):
        y_b = _dot(ys[pl.ds(b * tt + r, m), :], wso_ref[...])
        merged = mac_ref[b, pl.ds(r, m), :] + gb_ref[b, pl.ds(r, m), :] * y_b
        h1_ref[b, pl.ds(r, m), :] = h_ref[b, pl.ds(r, m), :] + _dot(merged.astype(BF16), wo_ref[...])

    for c in range(n_chunks):
        chunk(c)
        if seq_mode:
            finish(0, c * L, L)

    if seq_mode:
        @pl.when(is_last)
        def _():
            store_state(0)
    else:
        y_b = _dot(ys[...], wso_ref[...])
        merged = mac_ref[...].reshape(rows, D_MODEL) + gb_ref[...].reshape(rows, D_MODEL) * y_b
        h1 = h_ref[...].reshape(rows, D_MODEL) + _dot(merged.astype(BF16), wo_ref[...])
        h1_ref[...] = h1.reshape(nb, tt, D_MODEL)


def _ffn_kernel(h_ref, p_ref, stff_ref, gffn_ref, wup_ref, fw_ref, fb_ref, wdn_ref, gple_ref,
                wpg_ref, wpp_ref, gfin_ref,
                out_ref, nff_ref,
                xpad, nf_s, lhs, *, nb, tt, rc, final):
    t = pl.program_id(1)
    is_last = t == pl.num_programs(1) - 1
    rows = nb * tt
    cpb = tt // rc
    pff = _pad_rows(FF_KERNEL)
    half_blocks = FF_DIM // LANES
    slab_blocks = MXU_DIM // LANES
    n_slabs = FF_DIM // MXU_DIM

    @pl.when(t == 0)
    def _():
        _load_conv_tail(xpad, pff, FF_KERNEL, stff_ref)

    h1 = h_ref[...].reshape(rows, D_MODEL)
    nf_s[...] = _rms(h1, gffn_ref[...]).astype(BF16)

    def up_slab(j):
        for half in range(2):
            lo = half * FF_DIM + j * MXU_DIM
            _store_tile(xpad, pff, _dot(nf_s[...], wup_ref[:, lo:lo + MXU_DIM]), nb, tt, first_block=lo // LANES)

    def conv_slab(j):
        for i in range(rows // rc):
            b, r = _chunk_pos(i, cpb, rc)
            r2 = i * rc
            for lb in range(j * slab_blocks, (j + 1) * slab_blocks):
                halves = []
                for blk in (lb, half_blocks + lb):
                    cs = slice(blk * LANES, (blk + 1) * LANES)
                    acc = None
                    for k in range(FF_KERNEL):
                        term = xpad[b, blk, pl.ds(pff - (FF_KERNEL - 1) + k + r, rc), :] * fw_ref[k:k + 1, cs]
                        acc = term if acc is None else acc + term
                    halves.append(acc + fb_ref[:, cs])
                lhs[pl.ds(r2, rc), lb * LANES:(lb + 1) * LANES] = (_silu(halves[0]) * halves[1]).astype(BF16)

    up_slab(0)
    for j in range(n_slabs):
        if j + 1 < n_slabs:
            up_slab(j + 1)
        conv_slab(j)
    _carry_conv_tail(xpad, pff, FF_KERNEL, tt, nff_ref, is_last)

    h2 = h1 + _dot(lhs[...], wdn_ref[...])
    gate = _sigmoid(_dot(_rms(h2, gple_ref[...]).astype(BF16), wpg_ref[...]))
    pp = _dot(p_ref[...].reshape(rows, PLE_DIM).astype(BF16), wpp_ref[...])
    h3 = h2 + pp * gate
    if final:
        h3 = _rms(h3, gfin_ref[...])
    out_ref[...] = h3.reshape(nb, tt, D_MODEL)


def _tile_spec(nb, tt, c):
    return pl.BlockSpec((nb, tt, c), lambda b, t: (b, t, 0))


def _layer_tile_spec(layer, nb, tt, c):
    return pl.BlockSpec((None, nb, tt, c), lambda b, t: (layer, b, t, 0))


def _state_spec(nb, r, c):
    return pl.BlockSpec((nb, r, c), lambda b, t: (b, 0, 0))


def _layer_state_spec(layer, nb, r, c):
    return pl.BlockSpec((None, nb, r, c), lambda b, t: (layer, b, 0, 0))


def _const_spec(layer, arr):
    if layer is None:
        return pl.BlockSpec(arr.shape, lambda b, t: (0,) * arr.ndim, pipeline_mode=pl.Buffered(1))
    return pl.BlockSpec((None,) + arr.shape[1:], lambda b, t: (layer,) + (0,) * (arr.ndim - 1),
                        pipeline_mode=pl.Buffered(1))


def _params():
    return pltpu.CompilerParams(dimension_semantics=("arbitrary", "arbitrary"),
                                vmem_limit_bytes=VMEM_LIMIT)


def _conv_buffer(nb, k, tt, c):
    return pltpu.VMEM((nb, c // LANES, _pad_rows(k) + tt, LANES), F32)


def _ac_call(layer, h, st_sc, st_cf, w, nb, tt):
    bsz, tlen, _ = h.shape
    rc = min(tt, 32)
    rows = nb * tt
    consts = [w['g_mix'], w['w_ag'], w['w_cf'], w['sc_conv_w'], w['w_sc_out'],
              w['cf_conv_w'], w['cf_conv_b'], w['cf_ln_g'], w['cf_ln_b'], w['w_cf_out']]
    return pl.pallas_call(
        functools.partial(_ac_kernel, nb=nb, tt=tt, rc=rc),
        grid=(bsz // nb, tlen // tt),
        in_specs=[_tile_spec(nb, tt, D_MODEL), _layer_state_spec(layer, nb, SC_KERNEL - 1, D_MODEL),
                  _layer_state_spec(layer, nb, CF_KERNEL - 1, D_MODEL)] + [_const_spec(layer, a) for a in consts],
        out_specs=[_tile_spec(nb, tt, D_MODEL), _tile_spec(nb, tt, D_MODEL), _tile_spec(nb, tt, D_MODEL),
                   _state_spec(nb, SC_KERNEL - 1, D_MODEL), _state_spec(nb, CF_KERNEL - 1, D_MODEL)],
        out_shape=[jax.ShapeDtypeStruct((bsz, tlen, D_MODEL), BF16),
                   jax.ShapeDtypeStruct((bsz, tlen, D_MODEL), F32),
                   jax.ShapeDtypeStruct((bsz, tlen, D_MODEL), F32),
                   jax.ShapeDtypeStruct((bsz, SC_KERNEL - 1, D_MODEL), F32),
                   jax.ShapeDtypeStruct((bsz, CF_KERNEL - 1, D_MODEL), F32)],
        scratch_shapes=[_conv_buffer(nb, SC_KERNEL, tt, D_MODEL),
                        _conv_buffer(nb, CF_KERNEL, tt, D_MODEL),
                        pltpu.VMEM((OFF_Z // SLAB, rows, SLAB), F32),
                        pltpu.VMEM((rows, 2 * D_MODEL), F32),
                        pltpu.VMEM((rows, D_MODEL), F32),
                        pltpu.VMEM((rows, D_MODEL), BF16),
                        pltpu.VMEM((rows, D_MODEL), BF16),
                        pltpu.VMEM((rows, D_MODEL), F32),
                        pltpu.VMEM((rows, D_MODEL), F32)],
        compiler_params=_params(),
        name="mix_ac",
    )(h, st_sc, st_cf, *consts)


def _ssm_call(layer, n, h, mac, gb, st_mc, st_ssm, new_ssm, w, nb, tt):
    bsz, tlen, _ = h.shape
    rc = min(tt, 32)
    rows = nb * tt
    L = SSD_CHUNK
    prow = rows if tt >= L else nb * L
    consts = [w['w_z'], w['w_xbc'], w['w_dt'], w['ssm_conv_w'], w['ssm_conv_b'], w['dt_bias'], w['a_log'],
              w['d_exp'], w['ssm_norm_g'], w['w_ssm_out'], w['w_o']]
    inputs = [n, h, mac, gb, st_mc, st_ssm, *consts, w['head_expand']]
    in_specs = ([_tile_spec(nb, tt, D_MODEL)] * 4
                + [_layer_state_spec(layer, nb, SSM_CONV - 1, SSM_CONV_DIM),
                   _layer_state_spec(layer, nb, SSM_INNER, SSM_STATE)]
                + [_const_spec(layer, a) for a in consts] + [_const_spec(None, w['head_expand'])])
    aliases = {}
    if new_ssm is not None:
        aliases = {len(inputs): 2}
        inputs.append(new_ssm)
        in_specs.append(pl.BlockSpec(memory_space=pl.ANY))
    return pl.pallas_call(
        functools.partial(_ssm_kernel, nb=nb, tt=tt, rc=rc),
        grid=(bsz // nb, tlen // tt),
        in_specs=in_specs,
        out_specs=[_tile_spec(nb, tt, D_MODEL), _state_spec(nb, SSM_CONV - 1, SSM_CONV_DIM),
                   _layer_state_spec(layer, nb, SSM_INNER, SSM_STATE)],
        out_shape=[jax.ShapeDtypeStruct((bsz, tlen, D_MODEL), F32),
                   jax.ShapeDtypeStruct((bsz, SSM_CONV - 1, SSM_CONV_DIM), F32),
                   jax.ShapeDtypeStruct(st_ssm.shape, F32)],
        input_output_aliases=aliases,
        scratch_shapes=[_conv_buffer(nb, SSM_CONV, tt, SSM_CONV_DIM),
                        pltpu.VMEM((rows, SSM_INNER), F32),
                        pltpu.VMEM((prow, SSM_INNER), F32),
                        pltpu.VMEM((prow, SSM_GN), F32),
                        pltpu.VMEM((prow, SSM_GN), F32),
                        pltpu.VMEM((prow, LANES), F32),
                        pltpu.VMEM((L, LANES), F32),
                        pltpu.VMEM((LANES, L), F32),
                        pltpu.VMEM((SSM_STATE, SSM_INNER), F32),
                        pltpu.VMEM((rows, SSM_INNER), BF16)],
        compiler_params=_params(),
        name="mix_ssm",
    )(*inputs)


def _ffn_call(layer, h1, p, st_ff, w, g_final, nb, tt, final):
    bsz, tlen, _ = h1.shape
    rc = min(tt, 32)
    rows = nb * tt
    consts = [w['g_ffn'], w['w_up'], w['ff_conv_w'], w['ff_conv_b'], w['w_down'], w['g_ple'],
              w['w_ple_gate'], w['w_ple_proj']]
    return pl.pallas_call(
        functools.partial(_ffn_kernel, nb=nb, tt=tt, rc=rc, final=final),
        grid=(bsz // nb, tlen // tt),
        in_specs=[_tile_spec(nb, tt, D_MODEL), _layer_tile_spec(layer, nb, tt, PLE_DIM),
                  _layer_state_spec(layer, nb, FF_KERNEL - 1, 2 * FF_DIM)]
        + [_const_spec(layer, a) for a in consts] + [_const_spec(None, g_final)],
        out_specs=[_tile_spec(nb, tt, D_MODEL), _state_spec(nb, FF_KERNEL - 1, 2 * FF_DIM)],
        out_shape=[jax.ShapeDtypeStruct((bsz, tlen, D_MODEL), F32),
                   jax.ShapeDtypeStruct((bsz, FF_KERNEL - 1, 2 * FF_DIM), F32)],
        scratch_shapes=[_conv_buffer(nb, FF_KERNEL, tt, 2 * FF_DIM),
                        pltpu.VMEM((rows, D_MODEL), BF16),
                        pltpu.VMEM((rows, FF_DIM), BF16)],
        compiler_params=_params(),
        name="ffn_ple",
    )(h1, p, st_ff, *consts, g_final)


def _prep_weights(g_mix, w_in, sc_conv_w, w_sc_out, ssm_conv_w, ssm_conv_b, ssm_dt_bias, ssm_a_log, ssm_d,
                  ssm_norm_g, w_ssm_out, cf_conv_w, cf_conv_b, cf_ln_g, cf_ln_b, w_cf_out, w_o, g_ffn, w_up,
                  ff_conv_w, ff_conv_b, w_down, g_ple, w_ple_gate, w_ple_proj):
    depth = w_in.shape[0]
    row = lambda v: v.reshape(depth, 1, -1)
    lane_pad = lambda v: jnp.pad(v, [(0, 0)] * (v.ndim - 1) + [(0, LANES - v.shape[-1])])
    head_of_channel = jnp.arange(SSM_INNER) // SSM_HEAD_DIM
    term_lane = jnp.arange(LANES)
    return {
        'g_mix': row(g_mix),
        'w_ag': w_in[:, :, 0:OFF_Z].astype(BF16).reshape(depth, D_MODEL, OFF_Z // SLAB, SLAB).transpose(0, 2, 1, 3),
        'w_z': w_in[:, :, OFF_Z:OFF_XBC].astype(BF16),
        'w_xbc': w_in[:, :, OFF_XBC:OFF_DT].astype(BF16),
        'w_dt': lane_pad(w_in[:, :, OFF_DT:OFF_CF]).astype(BF16),
        'w_cf': w_in[:, :, OFF_CF:N_IN].astype(BF16),
        'sc_conv_w': sc_conv_w,
        'w_sc_out': w_sc_out.astype(BF16),
        'ssm_conv_w': ssm_conv_w,
        'ssm_conv_b': row(ssm_conv_b),
        'dt_bias': lane_pad(row(ssm_dt_bias)),
        'a_log': lane_pad(row(ssm_a_log)),
        'd_exp': jnp.repeat(ssm_d, SSM_HEAD_DIM, axis=-1).reshape(depth, 1, SSM_INNER),
        'ssm_norm_g': row(ssm_norm_g),
        'w_ssm_out': w_ssm_out.astype(BF16),
        'cf_conv_w': cf_conv_w,
        'cf_conv_b': row(cf_conv_b),
        'cf_ln_g': row(cf_ln_g),
        'cf_ln_b': row(cf_ln_b),
        'w_cf_out': w_cf_out.astype(BF16),
        'w_o': w_o.astype(BF16),
        'g_ffn': row(g_ffn),
        'w_up': w_up.astype(BF16),
        'ff_conv_w': ff_conv_w,
        'ff_conv_b': row(ff_conv_b),
        'w_down': w_down.astype(BF16),
        'g_ple': row(g_ple),
        'w_ple_gate': w_ple_gate.astype(BF16),
        'w_ple_proj': w_ple_proj.astype(BF16),
        'head_expand': ((term_lane[:, None] % SSM_HEADS == head_of_channel[None, :])
                        & (term_lane[:, None] < SPLIT_TERMS * SSM_HEADS)).astype(BF16),
    }


def _run_trunk(x, p, st_sc, st_mc, st_ssm, st_cf, st_ff, w, g_final, tiles):
    depth, bsz = st_ssm.shape[:2]
    st_ssm = st_ssm.reshape(depth, bsz, SSM_INNER, SSM_STATE)
    h = x
    new_ssm = None
    outs = [[], [], [], []]
    for i in range(depth):
        n, mac, gb, nsc, ncf = _ac_call(i, h, st_sc, st_cf, w, *tiles['ac'])
        h1, nmc, new_ssm = _ssm_call(i, n, h, mac, gb, st_mc, st_ssm, new_ssm, w, *tiles['ssm'])
        h, nff = _ffn_call(i, h1, p, st_ff, w, g_final, *tiles['ffn'], final=(i == depth - 1))
        for lst, s in zip(outs, (nsc, nmc, ncf, nff)):
            lst.append(s)
    nsc, nmc, ncf, nff = [jnp.stack(lst) for lst in outs]
    return h, [nsc, nmc, new_ssm.reshape(depth, bsz, SSM_HEADS, SSM_HEAD_DIM, SSM_STATE), ncf, nff]


def _tiles_for(bsz, tlen):
    if tlen >= 256:
        return {'ac': (1, 256), 'ssm': (1, 256), 'ffn': (1, 256)}
    return {'ac': (min(bsz, 16), tlen), 'ssm': (min(bsz, 4), tlen), 'ffn': (min(bsz, 16), tlen)}


def kernel(x_prompt, x_sample, p_prompt, p_sample, state_short_conv, state_ssm_conv, state_ssm, state_cf_conv, state_ffn_conv, g_mix, w_in, sc_conv_w, w_sc_out, ssm_conv_w, ssm_conv_b, ssm_dt_bias, ssm_a_log, ssm_d, ssm_norm_g, w_ssm_out, cf_conv_w, cf_conv_b, cf_ln_g, cf_ln_b, w_cf_out, w_o, g_ffn, w_up, ff_conv_w, ff_conv_b, w_down, g_ple, w_ple_gate, w_ple_proj, g_final):
    layers = _prep_weights(g_mix, w_in, sc_conv_w, w_sc_out, ssm_conv_w, ssm_conv_b, ssm_dt_bias, ssm_a_log,
                           ssm_d, ssm_norm_g, w_ssm_out, cf_conv_w, cf_conv_b, cf_ln_g, cf_ln_b, w_cf_out, w_o,
                           g_ffn, w_up, ff_conv_w, ff_conv_b, w_down, g_ple, w_ple_gate, w_ple_proj)
    gfin = g_final.reshape(1, D_MODEL)
    bp, tp, _ = x_prompt.shape
    bs, ts, _ = x_sample.shape
    z_sc = jnp.zeros((DEPTH, bp, SC_KERNEL - 1, D_MODEL), F32)
    z_mc = jnp.zeros((DEPTH, bp, SSM_CONV - 1, SSM_CONV_DIM), F32)
    z_ssm = jnp.zeros((DEPTH, bp, SSM_HEADS, SSM_HEAD_DIM, SSM_STATE), F32)
    z_cf = jnp.zeros((DEPTH, bp, CF_KERNEL - 1, D_MODEL), F32)
    z_ff = jnp.zeros((DEPTH, bp, FF_KERNEL - 1, 2 * FF_DIM), F32)
    y_p, sp = _run_trunk(x_prompt, p_prompt, z_sc, z_mc, z_ssm, z_cf, z_ff, layers, gfin, _tiles_for(bp, tp))
    y_s, ss = _run_trunk(x_sample, p_sample, state_short_conv, state_ssm_conv, state_ssm, state_cf_conv,
                         state_ffn_conv, layers, gfin, _tiles_for(bs, ts))
    return (y_p, y_s, sp[0], sp[1], sp[2], sp[3], sp[4], ss[0], ss[1], ss[2], ss[3], ss[4])
```

```python
import functools

import jax
import jax.numpy as jnp
from jax import lax
from jax.experimental import pallas as pl
from jax.experimental.pallas import tpu as pltpu

F32 = jnp.float32
BF16 = jnp.bfloat16

D_MODEL = 1024
DEPTH = 2
PLE_DIM = 256
EPS = 1e-6
SC_KERNEL = 3
SSM_INNER = 2 * D_MODEL
SSM_HEAD_DIM = 64
SSM_HEADS = SSM_INNER // SSM_HEAD_DIM
SSM_GROUPS = 4
SSM_STATE = 128
SSM_CONV = 4
SSM_GN = SSM_GROUPS * SSM_STATE
SSM_CONV_DIM = SSM_INNER + 2 * SSM_GN
CF_KERNEL = 31
FF_DIM = 2816
FF_KERNEL = 3
OFF_SC = 3 * D_MODEL
OFF_Z = OFF_SC + 3 * D_MODEL
OFF_XBC = OFF_Z + SSM_INNER
OFF_DT = OFF_XBC + SSM_CONV_DIM
OFF_CF = OFF_DT + SSM_HEADS
N_IN = OFF_CF + 2 * D_MODEL

LANES = 128
SUBLANES = 8
MXU_DIM = 256
SSD_CHUNK = 128
GROUP_W = SSM_INNER // SSM_GROUPS
SPLIT_TERMS = 3
SLAB = 2 * MXU_DIM
AC_STEPS = 4
VMEM_LIMIT = 56 * 1024 * 1024


def _dot(a, b):
    return jnp.dot(a, b, preferred_element_type=F32)


def _sigmoid(x):
    return 0.5 * jnp.tanh(0.5 * x) + 0.5


def _silu(x):
    h = 0.5 * x
    return h + h * jnp.tanh(h)


def _softplus(x):
    return jnp.maximum(x, 0.0) + jnp.log1p(jnp.exp(-jnp.abs(x)))


def _rms(x, g):
    return x * lax.rsqrt(jnp.mean(x * x, axis=-1, keepdims=True) + EPS) * g


def _pad_rows(k):
    return -(-(k - 1) // SUBLANES) * SUBLANES


def _chunk_pos(i, cpb, rc):
    return i // cpb, (i % cpb) * rc


def _for_each(n, body):
    for i in range(n):
        body(i)


def _slabs(dst, lhs_fn, w_ref):
    thunks = []
    for lo in range(0, w_ref.shape[1], SLAB):
        cs = slice(lo, min(lo + SLAB, w_ref.shape[1]))

        def run(cs=cs):
            dst[:, cs] = _dot(lhs_fn(), w_ref[:, cs])

        thunks.append(run)
    return thunks


def _interleave(n_steps, step, thunks):
    done = 0
    for i in range(n_steps):
        upto = (len(thunks) * (i + 1)) // n_steps
        for th in thunks[done:upto]:
            th()
        done = upto
        step(i)


def _lane_blocks(c):
    return [slice(lb * LANES, (lb + 1) * LANES) for lb in range(c // LANES)]


def _load_conv_tail(xpad, pad, k, st_ref):
    for lb, ls in enumerate(_lane_blocks(st_ref.shape[-1])):
        xpad[:, lb, pad - (k - 1):pad, :] = st_ref[:, :, ls]


def _carry_conv_tail(xpad, pad, k, tt, new_ref, is_last):
    tail = xpad[:, :, pad + tt - (k - 1):pad + tt, :]

    @pl.when(is_last)
    def _():
        for lb, ls in enumerate(_lane_blocks(new_ref.shape[-1])):
            new_ref[:, :, ls] = tail[:, lb]

    xpad[:, :, pad - (k - 1):pad, :] = tail


def _store_tile(xpad, pad, val, nb, tt, first_block=0):
    for lb, ls in enumerate(_lane_blocks(val.shape[-1])):
        xpad[:, first_block + lb, pad:pad + tt, :] = val[:, ls].reshape(nb, tt, LANES)


def _ac_kernel(h_ref, stsc_ref, stcf_ref, gmix_ref, wag_ref, wcf_ref, scw_ref, wsco_ref,
               cfw_ref, cfb_ref, lng_ref, lnb_ref, wcfo_ref,
               n_ref, mac_ref, gb_ref, nsc_ref, ncf_ref,
               xsc, xcf, proj_ag, proj_c, conv_c, lhs_a, lhs_c, ya, yc, *, nb, tt, rc):
    t = pl.program_id(1)
    is_last = t == pl.num_programs(1) - 1
    rows = nb * tt
    cpb = tt // rc
    chunks = rows // rc
    psc, pcf = _pad_rows(SC_KERNEL), _pad_rows(CF_KERNEL)
    n_slabs = wag_ref.shape[0]
    steps = AC_STEPS
    cps, sps = chunks // steps, n_slabs // steps

    @pl.when(t == 0)
    def _():
        _load_conv_tail(xsc, psc, SC_KERNEL, stsc_ref)
        _load_conv_tail(xcf, pcf, CF_KERNEL, stcf_ref)

    h = h_ref[...].reshape(rows, D_MODEL)
    n_ref[...] = _rms(h, gmix_ref[...]).astype(BF16).reshape(nb, tt, D_MODEL)

    def n_tile():
        return n_ref[...].reshape(rows, D_MODEL)

    def ag(rs, col, width=LANES):
        return proj_ag[col // SLAB, rs, col % SLAB:col % SLAB + width]

    proj_c[...] = _dot(n_tile(), wcf_ref[...])

    def c_fill(i):
        b, r = _chunk_pos(i, cpb, rc)
        r2 = i * rc
        for lb, ls in enumerate(_lane_blocks(D_MODEL)):
            c_g = proj_c[pl.ds(r2, rc), D_MODEL + lb * LANES:D_MODEL + (lb + 1) * LANES]
            xcf[b, lb, pl.ds(pcf + r, rc), :] = proj_c[pl.ds(r2, rc), ls] * _sigmoid(c_g)

    _for_each(chunks, c_fill)

    def c_step(s, carry):
        for q in range(sps):
            proj_ag[s * sps + q] = _dot(n_tile(), wag_ref[s * sps + q])
        for lb, ls in enumerate(_lane_blocks(D_MODEL)):
            taps = [jnp.broadcast_to(cfw_ref[k:k + 1, ls], (SUBLANES, LANES)) for k in range(CF_KERNEL)]
            for q in range(cps):
                i = s * cps + q
                b, r = _chunk_pos(i, cpb, rc)
                for rg in range(0, rc, SUBLANES):
                    acc = None
                    for k in range(CF_KERNEL):
                        term = xcf[b, lb, pl.ds(pcf - (CF_KERNEL - 1) + k + r + rg, SUBLANES), :] * taps[k]
                        acc = term if acc is None else acc + term
                    conv_c[pl.ds(pl.multiple_of(i * rc + rg, SUBLANES), SUBLANES), ls] = acc
        for q in range(cps):
            r2 = pl.multiple_of((s * cps + q) * rc, rc)
            v = conv_c[pl.ds(r2, rc), :] + cfb_ref[...]
            mu = jnp.mean(v, axis=-1, keepdims=True)
            vc = v - mu
            var = jnp.mean(vc * vc, axis=-1, keepdims=True)
            ln = vc * lax.rsqrt(var + EPS) * lng_ref[...] + lnb_ref[...]
            lhs_c[pl.ds(r2, rc), :] = _silu(ln).astype(BF16)
        return carry

    lax.fori_loop(0, steps, c_step, 0)
    _carry_conv_tail(xcf, pcf, CF_KERNEL, tt, ncf_ref, is_last)

    def a_fill(i):
        b, r = _chunk_pos(i, cpb, rc)
        rs = pl.ds(i * rc, rc)
        for lb, ls in enumerate(_lane_blocks(D_MODEL)):
            xsc[b, lb, pl.ds(psc + r, rc), :] = (ag(rs, OFF_SC + D_MODEL + lb * LANES)
                                                 * ag(rs, OFF_SC + 2 * D_MODEL + lb * LANES))

    _for_each(chunks, a_fill)

    def a_conv(i):
        b, r = _chunk_pos(i, cpb, rc)
        rs = pl.ds(i * rc, rc)
        for lb, ls in enumerate(_lane_blocks(D_MODEL)):
            u = None
            for k in range(SC_KERNEL):
                term = xsc[b, lb, pl.ds(psc - (SC_KERNEL - 1) + k + r, rc), :] * scw_ref[k:k + 1, ls]
                u = term if u is None else u + term
            lhs_a[rs, ls] = (ag(rs, OFF_SC + lb * LANES) * u).astype(BF16)

    _for_each(chunks, a_conv)
    _carry_conv_tail(xsc, psc, SC_KERNEL, tt, nsc_ref, is_last)
    yc[...] = _dot(lhs_c[...], wcfo_ref[...])
    ya[...] = _dot(lhs_a[...], wsco_ref[...])

    def g_merge(i):
        b, r = _chunk_pos(i, cpb, rc)
        rs = pl.ds(i * rc, rc)
        for lo in range(0, D_MODEL, SLAB):
            cs = slice(lo, lo + SLAB)
            g_a = _sigmoid(ag(rs, lo, SLAB))
            g_b = _sigmoid(ag(rs, D_MODEL + lo, SLAB))
            g_c = _sigmoid(ag(rs, 2 * D_MODEL + lo, SLAB))
            mac_ref[b, pl.ds(r, rc), cs] = g_a * ya[rs, cs] + g_c * yc[rs, cs]
            gb_ref[b, pl.ds(r, rc), cs] = g_b

    _for_each(chunks, g_merge)


def _ssm_kernel(n_ref, h_ref, mac_ref, gb_ref, stmc_ref, stssm_ref,
                wz_ref, wxbc_ref, wdt_ref, cw_ref, cb_ref, dtb_ref, alog_ref, de_ref, ng_ref,
                wso_ref, wo_ref, e_ref, *rest, nb, tt, rc):
    h1_ref, nmc_ref, nssm_ref, xpad, zs, xs_s, bs_s, cs_s, dt_s, acs_s, acst_s, ht, ys = rest[-13:]
    t = pl.program_id(1)
    is_last = t == pl.num_programs(1) - 1
    rows = nb * tt
    cpb = tt // rc
    pmc = _pad_rows(SSM_CONV)
    L = SSD_CHUNK
    seq_mode = tt >= L
    lv = L if seq_mode else tt
    n_chunks = rows // lv

    @pl.when(t == 0)
    def _():
        _load_conv_tail(xpad, pmc, SSM_CONV, stmc_ref)

    def load_state(b):
        for j in range(SSM_INNER // LANES):
            ht[:, j * LANES:(j + 1) * LANES] = stssm_ref[b, j * LANES:(j + 1) * LANES, :].T

    def store_state(b):
        for j in range(SSM_INNER // LANES):
            nssm_ref[b, j * LANES:(j + 1) * LANES, :] = ht[:, j * LANES:(j + 1) * LANES].T

    if seq_mode:
        @pl.when(t == 0)
        def _():
            load_state(0)
    else:
        xs_s[...] = jnp.zeros_like(xs_s)
        bs_s[...] = jnp.zeros_like(bs_s)
        cs_s[...] = jnp.zeros_like(cs_s)
        dt_s[...] = jnp.zeros_like(dt_s)

    def n_tile():
        return n_ref[...].reshape(rows, D_MODEL)

    _store_tile(xpad, pmc, _dot(n_tile(), wxbc_ref[...]), nb, tt)
    dt_all = _softplus(_dot(n_tile(), wdt_ref[...]) + dtb_ref[...])
    if seq_mode:
        dt_s[...] = dt_all
    else:
        for b in range(nb):
            dt_s[b * L:b * L + tt, :] = dt_all[b * tt:(b + 1) * tt, :]

    def conv(i):
        b, r = _chunk_pos(i, cpb, rc)
        dst = i * (rc if seq_mode else L)
        for lb, ls in enumerate(_lane_blocks(SSM_CONV_DIM)):
            acc = None
            for k in range(SSM_CONV):
                term = xpad[b, lb, pl.ds(pmc - (SSM_CONV - 1) + k + r, rc), :] * cw_ref[k:k + 1, ls]
                acc = term if acc is None else acc + term
            xbc = _silu(acc + cb_ref[:, ls])
            if ls.start < SSM_INNER:
                xs_s[pl.ds(dst, rc), ls] = xbc
            elif ls.start < SSM_INNER + SSM_GN:
                bs_s[pl.ds(dst, rc), ls.start - SSM_INNER:ls.stop - SSM_INNER] = xbc
            else:
                cs_s[pl.ds(dst, rc), ls.start - SSM_INNER - SSM_GN:ls.stop - SSM_INNER - SSM_GN] = xbc

    _interleave(rows // rc, conv, _slabs(zs, n_tile, wz_ref))
    _carry_conv_tail(xpad, pmc, SSM_CONV, tt, nmc_ref, is_last)

    a_row = -jnp.exp(alog_ref[...])
    ii = lax.broadcasted_iota(jnp.int32, (L, L), 0)
    jj = lax.broadcasted_iota(jnp.int32, (L, L), 1)
    causal = ii >= jj
    tril = jnp.where(causal, 1.0, 0.0).astype(BF16)
    low_half = (lax.broadcasted_iota(jnp.int32, (L, GROUP_W), 1) % LANES) < SSM_HEAD_DIM

    def pack_terms(x):
        head_lane = lax.broadcasted_iota(jnp.int32, x.shape, 1) < SSM_HEADS
        r = jnp.where(head_lane, x, 0.0)
        packed = None
        for k in range(SPLIT_TERMS):
            p = r.astype(BF16).astype(F32)
            placed = p if k == 0 else pltpu.roll(p, k * SSM_HEADS, 1)
            packed = placed if packed is None else packed + placed
            r = r - p
        return packed.astype(BF16)

    def expand(x):
        return _dot(pack_terms(x), e_ref[...])

    def chunk(c):
        if not seq_mode:
            load_state(c)
        r0 = c * L
        rv = c * lv
        dtc = dt_s[pl.ds(r0, L), :]
        cs3 = _dot(tril, pack_terms(dtc * a_row))
        acs = cs3
        for k in range(1, SPLIT_TERMS):
            acs = acs + pltpu.roll(cs3, LANES - k * SSM_HEADS, 1)
        acs_s[...] = acs
        acst_s[...] = acs.T
        a_last = acs[L - 1:L, :]
        dt_e = expand(dtc)
        eacs_e = expand(jnp.exp(acs))
        dec_e = expand(jnp.exp(a_last - acs))
        elast_e = expand(jnp.broadcast_to(jnp.exp(a_last), (SUBLANES, LANES)))[0:1, :]
        for g in range(SSM_GROUPS):
            gs = slice(g * GROUP_W, (g + 1) * GROUP_W)
            ns = slice(g * SSM_STATE, (g + 1) * SSM_STATE)
            xs_g = xs_s[pl.ds(r0, L), gs]
            xdt = xs_g * dt_e[:, gs]
            x_even = jnp.where(low_half, xdt, 0.0).astype(BF16)
            x_odd = jnp.where(low_half, 0.0, xdt).astype(BF16)
            xdw_b = (xdt * dec_e[:, gs]).astype(BF16)
            c_g = cs_s[pl.ds(r0, L), ns].astype(BF16)
            bt_g = bs_s[pl.ds(r0, L), ns].T.astype(BF16)
            cb = _dot(c_g, bt_g)
            ht_g = ht[:, gs]
            y_off = _dot(c_g, ht_g.astype(BF16)) * eacs_e[:, gs]
            ht[:, gs] = ht_g * elast_e[:, gs] + _dot(bt_g, xdw_b)
            y_cols = []
            for pr in range(GROUP_W // LANES):
                hd = g * (GROUP_W // SSM_HEAD_DIM) + 2 * pr
                ps = slice(pr * LANES, (pr + 1) * LANES)
                ms = []
                for hh in (hd, hd + 1):
                    diff = acs_s[:, hh:hh + 1] - acst_s[hh:hh + 1, :]
                    ms.append((cb * jnp.exp(jnp.where(causal, diff, -jnp.inf))).astype(BF16))
                y_cols.append(_dot(jnp.concatenate(ms, axis=1),
                                   jnp.concatenate([x_even[:, ps], x_odd[:, ps]], axis=0)))
            y = jnp.concatenate(y_cols, axis=-1) + y_off + de_ref[:, gs] * xs_g
            y = y[0:lv, :] * _silu(zs[pl.ds(rv, lv), gs])
            y = y * lax.rsqrt(jnp.mean(y * y, axis=-1, keepdims=True) + EPS)
            ys[pl.ds(rv, lv), gs] = (y * ng_ref[:, gs]).astype(BF16)
        if not seq_mode:
            store_state(c)

    def finish(b, r, m):
        y_b = _dot(ys[pl.ds(b * tt + r, m), :], wso_ref[...])
        merged = mac_ref[b, pl.ds(r, m), :] + gb_ref[b, pl.ds(r, m), :] * y_b
        h1_ref[b, pl.ds(r, m), :] = h_ref[b, pl.ds(r, m), :] + _dot(merged.astype(BF16), wo_ref[...])

    for c in range(n_chunks):
        chunk(c)
        if seq_mode:
            finish(0, c * L, L)

    if seq_mode:
        @pl.when(is_last)
        def _():
            store_state(0)
    else:
        y_b = _dot(ys[...], wso_ref[...])
        merged = mac_ref[...].reshape(rows, D_MODEL) + gb_ref[...].reshape(rows, D_MODEL) * y_b
        h1 = h_ref[...].reshape(rows, D_MODEL) + _dot(merged.astype(BF16), wo_ref[...])
        h1_ref[...] = h1.reshape(nb, tt, D_MODEL)


def _ffn_kernel(h_ref, p_ref, stff_ref, gffn_ref, wup_ref, fw_ref, fb_ref, wdn_ref, gple_ref,
                wpg_ref, wpp_ref, gfin_ref,
                out_ref, nff_ref,
                xpad, nf_s, lhs, *, nb, tt, rc, final):
    t = pl.program_id(1)
    is_last = t == pl.num_programs(1) - 1
    rows = nb * tt
    cpb = tt // rc
    pff = _pad_rows(FF_KERNEL)
    half_blocks = FF_DIM // LANES
    slab_blocks = MXU_DIM // LANES
    n_slabs = FF_DIM // MXU_DIM

    @pl.when(t == 0)
    def _():
        _load_conv_tail(xpad, pff, FF_KERNEL, stff_ref)

    h1 = h_ref[...].reshape(rows, D_MODEL)
    nf_s[...] = _rms(h1, gffn_ref[...]).astype(BF16)

    def up_slab(j):
        for half in range(2):
            lo = half * FF_DIM + j * MXU_DIM
            _store_tile(xpad, pff, _dot(nf_s[...], wup_ref[:, lo:lo + MXU_DIM]), nb, tt, first_block=lo // LANES)

    def conv_slab(j):
        for i in range(rows // rc):
            b, r = _chunk_pos(i, cpb, rc)
            r2 = i * rc
            for lb in range(j * slab_blocks, (j + 1) * slab_blocks):
                halves = []
                for blk in (lb, half_blocks + lb):
                    cs = slice(blk * LANES, (blk + 1) * LANES)
                    acc = None
                    for k in range(FF_KERNEL):
                        term = xpad[b, blk, pl.ds(pff - (FF_KERNEL - 1) + k + r, rc), :] * fw_ref[k:k + 1, cs]
                        acc = term if acc is None else acc + term
                    halves.append(acc + fb_ref[:, cs])
                lhs[pl.ds(r2, rc), lb * LANES:(lb + 1) * LANES] = (_silu(halves[0]) * halves[1]).astype(BF16)

    up_slab(0)
    for j in range(n_slabs):
        if j + 1 < n_slabs:
            up_slab(j + 1)
        conv_slab(j)
    _carry_conv_tail(xpad, pff, FF_KERNEL, tt, nff_ref, is_last)

    h2 = h1 + _dot(lhs[...], wdn_ref[...])
    gate = _sigmoid(_dot(_rms(h2, gple_ref[...]).astype(BF16), wpg_ref[...]))
    pp = _dot(p_ref[...].reshape(rows, PLE_DIM).astype(BF16), wpp_ref[...])
    h3 = h2 + pp * gate
    if final:
        h3 = _rms(h3, gfin_ref[...])
    out_ref[...] = h3.reshape(nb, tt, D_MODEL)


def _tile_spec(nb, tt, c):
    return pl.BlockSpec((nb, tt, c), lambda b, t: (b, t, 0))


def _layer_tile_spec(layer, nb, tt, c):
    return pl.BlockSpec((None, nb, tt, c), lambda b, t: (layer, b, t, 0))


def _state_spec(nb, r, c):
    return pl.BlockSpec((nb, r, c), lambda b, t: (b, 0, 0))


def _layer_state_spec(layer, nb, r, c):
    return pl.BlockSpec((None, nb, r, c), lambda b, t: (layer, b, 0, 0))


def _const_spec(layer, arr):
    if layer is None:
        return pl.BlockSpec(arr.shape, lambda b, t: (0,) * arr.ndim, pipeline_mode=pl.Buffered(1))
    return pl.BlockSpec((None,) + arr.shape[1:], lambda b, t: (layer,) + (0,) * (arr.ndim - 1),
                        pipeline_mode=pl.Buffered(1))


def _params():
    return pltpu.CompilerParams(dimension_semantics=("arbitrary", "arbitrary"),
                                vmem_limit_bytes=VMEM_LIMIT)


def _conv_buffer(nb, k, tt, c):
    return pltpu.VMEM((nb, c // LANES, _pad_rows(k) + tt, LANES), F32)


def _ac_call(layer, h, st_sc, st_cf, w, nb, tt):
    bsz, tlen, _ = h.shape
    rc = min(tt, 32)
    rows = nb * tt
    consts = [w['g_mix'], w['w_ag'], w['w_cf'], w['sc_conv_w'], w['w_sc_out'],
              w['cf_conv_w'], w['cf_conv_b'], w['cf_ln_g'], w['cf_ln_b'], w['w_cf_out']]
    return pl.pallas_call(
        functools.partial(_ac_kernel, nb=nb, tt=tt, rc=rc),
        grid=(bsz // nb, tlen // tt),
        in_specs=[_tile_spec(nb, tt, D_MODEL), _layer_state_spec(layer, nb, SC_KERNEL - 1, D_MODEL),
                  _layer_state_spec(layer, nb, CF_KERNEL - 1, D_MODEL)] + [_const_spec(layer, a) for a in consts],
        out_specs=[_tile_spec(nb, tt, D_MODEL), _tile_spec(nb, tt, D_MODEL), _tile_spec(nb, tt, D_MODEL),
                   _state_spec(nb, SC_KERNEL - 1, D_MODEL), _state_spec(nb, CF_KERNEL - 1, D_MODEL)],
        out_shape=[jax.ShapeDtypeStruct((bsz, tlen, D_MODEL), BF16),
                   jax.ShapeDtypeStruct((bsz, tlen, D_MODEL), F32),
                   jax.ShapeDtypeStruct((bsz, tlen, D_MODEL), F32),
                   jax.ShapeDtypeStruct((bsz, SC_KERNEL - 1, D_MODEL), F32),
                   jax.ShapeDtypeStruct((bsz, CF_KERNEL - 1, D_MODEL), F32)],
        scratch_shapes=[_conv_buffer(nb, SC_KERNEL, tt, D_MODEL),
                        _conv_buffer(nb, CF_KERNEL, tt, D_MODEL),
                        pltpu.VMEM((OFF_Z // SLAB, rows, SLAB), F32),
                        pltpu.VMEM((rows, 2 * D_MODEL), F32),
                        pltpu.VMEM((rows, D_MODEL), F32),
                        pltpu.VMEM((rows, D_MODEL), BF16),
                        pltpu.VMEM((rows, D_MODEL), BF16),
                        pltpu.VMEM((rows, D_MODEL), F32),
                        pltpu.VMEM((rows, D_MODEL), F32)],
        compiler_params=_params(),
        name="mix_ac",
    )(h, st_sc, st_cf, *consts)


def _ssm_call(layer, n, h, mac, gb, st_mc, st_ssm, new_ssm, w, nb, tt):
    bsz, tlen, _ = h.shape
    rc = min(tt, 32)
    rows = nb * tt
    L = SSD_CHUNK
    prow = rows if tt >= L else nb * L
    consts = [w['w_z'], w['w_xbc'], w['w_dt'], w['ssm_conv_w'], w['ssm_conv_b'], w['dt_bias'], w['a_log'],
              w['d_exp'], w['ssm_norm_g'], w['w_ssm_out'], w['w_o']]
    inputs = [n, h, mac, gb, st_mc, st_ssm, *consts, w['head_expand']]
    in_specs = ([_tile_spec(nb, tt, D_MODEL)] * 4
                + [_layer_state_spec(layer, nb, SSM_CONV - 1, SSM_CONV_DIM),
                   _layer_state_spec(layer, nb, SSM_INNER, SSM_STATE)]
                + [_const_spec(layer, a) for a in consts] + [_const_spec(None, w['head_expand'])])
    aliases = {}
    if new_ssm is not None:
        aliases = {len(inputs): 2}
        inputs.append(new_ssm)
        in_specs.append(pl.BlockSpec(memory_space=pl.ANY))
    return pl.pallas_call(
        functools.partial(_ssm_kernel, nb=nb, tt=tt, rc=rc),
        grid=(bsz // nb, tlen // tt),
        in_specs=in_specs,
        out_specs=[_tile_spec(nb, tt, D_MODEL), _state_spec(nb, SSM_CONV - 1, SSM_CONV_DIM),
                   _layer_state_spec(layer, nb, SSM_INNER, SSM_STATE)],
        out_shape=[jax.ShapeDtypeStruct((bsz, tlen, D_MODEL), F32),
                   jax.ShapeDtypeStruct((bsz, SSM_CONV - 1, SSM_CONV_DIM), F32),
                   jax.ShapeDtypeStruct(st_ssm.shape, F32)],
        input_output_aliases=aliases,
        scratch_shapes=[_conv_buffer(nb, SSM_CONV, tt, SSM_CONV_DIM),
                        pltpu.VMEM((rows, SSM_INNER), F32),
                        pltpu.VMEM((prow, SSM_INNER), F32),
                        pltpu.VMEM((prow, SSM_GN), F32),
                        pltpu.VMEM((prow, SSM_GN), F32),
                        pltpu.VMEM((prow, LANES), F32),
                        pltpu.VMEM((L, LANES), F32),
                        pltpu.VMEM((LANES, L), F32),
                        pltpu.VMEM((SSM_STATE, SSM_INNER), F32),
                        pltpu.VMEM((rows, SSM_INNER), BF16)],
        compiler_params=_params(),
        name="mix_ssm",
    )(*inputs)


def _ffn_call(layer, h1, p, st_ff, w, g_final, nb, tt, final):
    bsz, tlen, _ = h1.shape
    rc = min(tt, 32)
    rows = nb * tt
    consts = [w['g_ffn'], w['w_up'], w['ff_conv_w'], w['ff_conv_b'], w['w_down'], w['g_ple'],
              w['w_ple_gate'], w['w_ple_proj']]
    return pl.pallas_call(
        functools.partial(_ffn_kernel, nb=nb, tt=tt, rc=rc, final=final),
        grid=(bsz // nb, tlen // tt),
        in_specs=[_tile_spec(nb, tt, D_MODEL), _layer_tile_spec(layer, nb, tt, PLE_DIM),
                  _layer_state_spec(layer, nb, FF_KERNEL - 1, 2 * FF_DIM)]
        + [_const_spec(layer, a) for a in consts] + [_const_spec(None, g_final)],
        out_specs=[_tile_spec(nb, tt, D_MODEL), _state_spec(nb, FF_KERNEL - 1, 2 * FF_DIM)],
        out_shape=[jax.ShapeDtypeStruct((bsz, tlen, D_MODEL), F32),
                   jax.ShapeDtypeStruct((bsz, FF_KERNEL - 1, 2 * FF_DIM), F32)],
        scratch_shapes=[_conv_buffer(nb, FF_KERNEL, tt, 2 * FF_DIM),
                        pltpu.VMEM((rows, D_MODEL), BF16),
                        pltpu.VMEM((rows, FF_DIM), BF16)],
        compiler_params=_params(),
        name="ffn_ple",
    )(h1, p, st_ff, *consts, g_final)


def _prep_weights(g_mix, w_in, sc_conv_w, w_sc_out, ssm_conv_w, ssm_conv_b, ssm_dt_bias, ssm_a_log, ssm_d,
                  ssm_norm_g, w_ssm_out, cf_conv_w, cf_conv_b, cf_ln_g, cf_ln_b, w_cf_out, w_o, g_ffn, w_up,
                  ff_conv_w, ff_conv_b, w_down, g_ple, w_ple_gate, w_ple_proj):
    depth = w_in.shape[0]
    row = lambda v: v.reshape(depth, 1, -1)
    lane_pad = lambda v: jnp.pad(v, [(0, 0)] * (v.ndim - 1) + [(0, LANES - v.shape[-1])])
    head_of_channel = jnp.arange(SSM_INNER) // SSM_HEAD_DIM
    term_lane = jnp.arange(LANES)
    return {
        'g_mix': row(g_mix),
        'w_ag': w_in[:, :, 0:OFF_Z].astype(BF16).reshape(depth, D_MODEL, OFF_Z // SLAB, SLAB).transpose(0, 2, 1, 3),
        'w_z': w_in[:, :, OFF_Z:OFF_XBC].astype(BF16),
        'w_xbc': w_in[:, :, OFF_XBC:OFF_DT].astype(BF16),
        'w_dt': lane_pad(w_in[:, :, OFF_DT:OFF_CF]).astype(BF16),
        'w_cf': w_in[:, :, OFF_CF:N_IN].astype(BF16),
        'sc_conv_w': sc_conv_w,
        'w_sc_out': w_sc_out.astype(BF16),
        'ssm_conv_w': ssm_conv_w,
        'ssm_conv_b': row(ssm_conv_b),
        'dt_bias': lane_pad(row(ssm_dt_bias)),
        'a_log': lane_pad(row(ssm_a_log)),
        'd_exp': jnp.repeat(ssm_d, SSM_HEAD_DIM, axis=-1).reshape(depth, 1, SSM_INNER),
        'ssm_norm_g': row(ssm_norm_g),
        'w_ssm_out': w_ssm_out.astype(BF16),
        'cf_conv_w': cf_conv_w,
        'cf_conv_b': row(cf_conv_b),
        'cf_ln_g': row(cf_ln_g),
        'cf_ln_b': row(cf_ln_b),
        'w_cf_out': w_cf_out.astype(BF16),
        'w_o': w_o.astype(BF16),
        'g_ffn': row(g_ffn),
        'w_up': w_up.astype(BF16),
        'ff_conv_w': ff_conv_w,
        'ff_conv_b': row(ff_conv_b),
        'w_down': w_down.astype(BF16),
        'g_ple': row(g_ple),
        'w_ple_gate': w_ple_gate.astype(BF16),
        'w_ple_proj': w_ple_proj.astype(BF16),
        'head_expand': ((term_lane[:, None] % SSM_HEADS == head_of_channel[None, :])
                        & (term_lane[:, None] < SPLIT_TERMS * SSM_HEADS)).astype(BF16),
    }


def _run_trunk(x, p, st_sc, st_mc, st_ssm, st_cf, st_ff, w, g_final, tiles):
    depth, bsz = st_ssm.shape[:2]
    st_ssm = st_ssm.reshape(depth, bsz, SSM_INNER, SSM_STATE)
    h = x
    new_ssm = None
    outs = [[], [], [], []]
    for i in range(depth):
        n, mac, gb, nsc, ncf = _ac_call(i, h, st_sc, st_cf, w, *tiles['ac'])
        h1, nmc, new_ssm = _ssm_call(i, n, h, mac, gb, st_mc, st_ssm, new_ssm, w, *tiles['ssm'])
        h, nff = _ffn_call(i, h1, p, st_ff, w, g_final, *tiles['ffn'], final=(i == depth - 1))
        for lst, s in zip(outs, (nsc, nmc, ncf, nff)):
            lst.append(s)
    nsc, nmc, ncf, nff = [jnp.stack(lst) for lst in outs]
    return h, [nsc, nmc, new_ssm.reshape(depth, bsz, SSM_HEADS, SSM_HEAD_DIM, SSM_STATE), ncf, nff]


def _tiles_for(bsz, tlen):
    if tlen >= 256:
        return {'ac': (1, 256), 'ssm': (1, 256), 'ffn': (1, 256)}
    return {'ac': (min(bsz, 16), tlen), 'ssm': (min(bsz, 4), tlen), 'ffn': (min(bsz, 16), tlen)}


def kernel(x_prompt, x_sample, p_prompt, p_sample, state_short_conv, state_ssm_conv, state_ssm, state_cf_conv, state_ffn_conv, g_mix, w_in, sc_conv_w, w_sc_out, ssm_conv_w, ssm_conv_b, ssm_dt_bias, ssm_a_log, ssm_d, ssm_norm_g, w_ssm_out, cf_conv_w, cf_conv_b, cf_ln_g, cf_ln_b, w_cf_out, w_o, g_ffn, w_up, ff_conv_w, ff_conv_b, w_down, g_ple, w_ple_gate, w_ple_proj, g_final):
    layers = _prep_weights(g_mix, w_in, sc_conv_w, w_sc_out, ssm_conv_w, ssm_conv_b, ssm_dt_bias, ssm_a_log,
                           ssm_d, ssm_norm_g, w_ssm_out, cf_conv_w, cf_conv_b, cf_ln_g, cf_ln_b, w_cf_out, w_o,
                           g_ffn, w_up, ff_conv_w, ff_conv_b, w_down, g_ple, w_ple_gate, w_ple_proj)
    gfin = g_final.reshape(1, D_MODEL)
    bp, tp, _ = x_prompt.shape
    bs, ts, _ = x_sample.shape
    z_sc = jnp.zeros((DEPTH, bp, SC_KERNEL - 1, D_MODEL), F32)
    z_mc = jnp.zeros((DEPTH, bp, SSM_CONV - 1, SSM_CONV_DIM), F32)
    z_ssm = jnp.zeros((DEPTH, bp, SSM_HEADS, SSM_HEAD_DIM, SSM_STATE), F32)
    z_cf = jnp.zeros((DEPTH, bp, CF_KERNEL - 1, D_MODEL), F32)
    z_ff = jnp.zeros((DEPTH, bp, FF_KERNEL - 1, 2 * FF_DIM), F32)
    y_p, sp = _run_trunk(x_prompt, p_prompt, z_sc, z_mc, z_ssm, z_cf, z_ff, layers, gfin, _tiles_for(bp, tp))
    y_s, ss = _run_trunk(x_sample, p_sample, state_short_conv, state_ssm_conv, state_ssm, state_cf_conv,
                         state_ffn_conv, layers, gfin, _tiles_for(bs, ts))
    return (y_p, y_s, sp[0], sp[1], sp[2], sp[3], sp[4], ss[0], ss[1], ss[2], ss[3], ss[4])
```

```python
import functools

import jax
import jax.numpy as jnp
from jax import lax
from jax.experimental import pallas as pl
from jax.experimental.pallas import tpu as pltpu

F32 = jnp.float32
BF16 = jnp.bfloat16

D_MODEL = 1024
DEPTH = 2
PLE_DIM = 256
EPS = 1e-6
SC_KERNEL = 3
SSM_INNER = 2 * D_MODEL
SSM_HEAD_DIM = 64
SSM_HEADS = SSM_INNER // SSM_HEAD_DIM
SSM_GROUPS = 4
SSM_STATE = 128
SSM_CONV = 4
SSM_GN = SSM_GROUPS * SSM_STATE
SSM_CONV_DIM = SSM_INNER + 2 * SSM_GN
CF_KERNEL = 31
FF_DIM = 2816
FF_KERNEL = 3
OFF_SC = 3 * D_MODEL
OFF_Z = OFF_SC + 3 * D_MODEL
OFF_XBC = OFF_Z + SSM_INNER
OFF_DT = OFF_XBC + SSM_CONV_DIM
OFF_CF = OFF_DT + SSM_HEADS
N_IN = OFF_CF + 2 * D_MODEL

LANES = 128
SUBLANES = 8
MXU_DIM = 256
SSD_CHUNK = 128
GROUP_W = SSM_INNER // SSM_GROUPS
SPLIT_TERMS = 3
SLAB = 2 * MXU_DIM
AC_STEPS = 4
VMEM_LIMIT = 56 * 1024 * 1024
LOG2_E = 1.4426950408889634


def _dot(a, b):
    return jnp.dot(a, b, preferred_element_type=F32)


def _sigmoid(x):
    return 0.5 * jnp.tanh(0.5 * x) + 0.5


def _silu(x):
    h = 0.5 * x
    return h + h * jnp.tanh(h)


def _softplus(x):
    return jnp.maximum(x, 0.0) + jnp.log1p(jnp.exp(-jnp.abs(x)))


def _rms(x, g):
    return x * lax.rsqrt(jnp.mean(x * x, axis=-1, keepdims=True) + EPS) * g


def _pad_rows(k):
    return -(-(k - 1) // SUBLANES) * SUBLANES


def _chunk_pos(i, cpb, rc):
    return i // cpb, (i % cpb) * rc


def _for_each(n, body):
    for i in range(n):
        body(i)


def _slabs(dst, lhs_fn, w_ref):
    thunks = []
    for lo in range(0, w_ref.shape[1], SLAB):
        cs = slice(lo, min(lo + SLAB, w_ref.shape[1]))

        def run(cs=cs):
            dst[:, cs] = _dot(lhs_fn(), w_ref[:, cs])

        thunks.append(run)
    return thunks


def _interleave(n_steps, step, thunks):
    done = 0
    for i in range(n_steps):
        upto = (len(thunks) * (i + 1)) // n_steps
        for th in thunks[done:upto]:
            th()
        done = upto
        step(i)


def _lane_blocks(c):
    return [slice(lb * LANES, (lb + 1) * LANES) for lb in range(c // LANES)]


def _load_conv_tail(xpad, pad, k, st_ref):
    for lb, ls in enumerate(_lane_blocks(st_ref.shape[-1])):
        xpad[:, lb, pad - (k - 1):pad, :] = st_ref[:, :, ls]


def _carry_conv_tail(xpad, pad, k, tt, new_ref, is_last):
    tail = xpad[:, :, pad + tt - (k - 1):pad + tt, :]

    @pl.when(is_last)
    def _():
        for lb, ls in enumerate(_lane_blocks(new_ref.shape[-1])):
            new_ref[:, :, ls] = tail[:, lb]

    xpad[:, :, pad - (k - 1):pad, :] = tail


def _store_tile(xpad, pad, val, nb, tt, first_block=0):
    for lb, ls in enumerate(_lane_blocks(val.shape[-1])):
        xpad[:, first_block + lb, pad:pad + tt, :] = val[:, ls].reshape(nb, tt, LANES)


def _ac_kernel(h_ref, stsc_ref, stcf_ref, gmix_ref, wag_ref, wcf_ref, scw_ref, wsco_ref,
               cfw_ref, cfb_ref, lng_ref, lnb_ref, wcfo_ref,
               n_ref, mac_ref, gb_ref, nsc_ref, ncf_ref,
               xsc, xcf, proj_ag, proj_c, conv_c, lhs_a, lhs_c, ya, yc, *, nb, tt, rc):
    t = pl.program_id(1)
    is_last = t == pl.num_programs(1) - 1
    rows = nb * tt
    cpb = tt // rc
    chunks = rows // rc
    psc, pcf = _pad_rows(SC_KERNEL), _pad_rows(CF_KERNEL)
    n_slabs = wag_ref.shape[0]
    steps = AC_STEPS
    cps, sps = chunks // steps, n_slabs // steps

    @pl.when(t == 0)
    def _():
        _load_conv_tail(xsc, psc, SC_KERNEL, stsc_ref)
        _load_conv_tail(xcf, pcf, CF_KERNEL, stcf_ref)

    h = h_ref[...].reshape(rows, D_MODEL)
    n_ref[...] = _rms(h, gmix_ref[...]).astype(BF16).reshape(nb, tt, D_MODEL)

    def n_tile():
        return n_ref[...].reshape(rows, D_MODEL)

    def ag(rs, col, width=LANES):
        return proj_ag[col // SLAB, rs, col % SLAB:col % SLAB + width]

    proj_c[...] = _dot(n_tile(), wcf_ref[...])

    def c_fill(i):
        b, r = _chunk_pos(i, cpb, rc)
        r2 = i * rc
        for lb, ls in enumerate(_lane_blocks(D_MODEL)):
            c_g = proj_c[pl.ds(r2, rc), D_MODEL + lb * LANES:D_MODEL + (lb + 1) * LANES]
            xcf[b, lb, pl.ds(pcf + r, rc), :] = proj_c[pl.ds(r2, rc), ls] * _sigmoid(c_g)

    _for_each(chunks, c_fill)

    def c_step(s, carry):
        for q in range(sps):
            proj_ag[s * sps + q] = _dot(n_tile(), wag_ref[s * sps + q])
        for lb, ls in enumerate(_lane_blocks(D_MODEL)):
            taps = [jnp.broadcast_to(cfw_ref[k:k + 1, ls], (SUBLANES, LANES)) for k in range(CF_KERNEL)]
            for q in range(cps):
                i = s * cps + q
                b, r = _chunk_pos(i, cpb, rc)
                for rg in range(0, rc, SUBLANES):
                    acc = None
                    for k in range(CF_KERNEL):
                        term = xcf[b, lb, pl.ds(pcf - (CF_KERNEL - 1) + k + r + rg, SUBLANES), :] * taps[k]
                        acc = term if acc is None else acc + term
                    conv_c[pl.ds(pl.multiple_of(i * rc + rg, SUBLANES), SUBLANES), ls] = acc
        for q in range(cps):
            r2 = pl.multiple_of((s * cps + q) * rc, rc)
            v = conv_c[pl.ds(r2, rc), :] + cfb_ref[...]
            mu = jnp.mean(v, axis=-1, keepdims=True)
            vc = v - mu
            var = jnp.mean(vc * vc, axis=-1, keepdims=True)
            ln = vc * lax.rsqrt(var + EPS) * lng_ref[...] + lnb_ref[...]
            lhs_c[pl.ds(r2, rc), :] = _silu(ln).astype(BF16)
        return carry

    lax.fori_loop(0, steps, c_step, 0)
    _carry_conv_tail(xcf, pcf, CF_KERNEL, tt, ncf_ref, is_last)

    def a_fill(i):
        b, r = _chunk_pos(i, cpb, rc)
        rs = pl.ds(i * rc, rc)
        for lb, ls in enumerate(_lane_blocks(D_MODEL)):
            xsc[b, lb, pl.ds(psc + r, rc), :] = (ag(rs, OFF_SC + D_MODEL + lb * LANES)
                                                 * ag(rs, OFF_SC + 2 * D_MODEL + lb * LANES))

    _for_each(chunks, a_fill)

    def a_conv(i):
        b, r = _chunk_pos(i, cpb, rc)
        rs = pl.ds(i * rc, rc)
        for lb, ls in enumerate(_lane_blocks(D_MODEL)):
            u = None
            for k in range(SC_KERNEL):
                term = xsc[b, lb, pl.ds(psc - (SC_KERNEL - 1) + k + r, rc), :] * scw_ref[k:k + 1, ls]
                u = term if u is None else u + term
            lhs_a[rs, ls] = (ag(rs, OFF_SC + lb * LANES) * u).astype(BF16)

    _for_each(chunks, a_conv)
    _carry_conv_tail(xsc, psc, SC_KERNEL, tt, nsc_ref, is_last)
    yc[...] = _dot(lhs_c[...], wcfo_ref[...])
    ya[...] = _dot(lhs_a[...], wsco_ref[...])

    def g_merge(i):
        b, r = _chunk_pos(i, cpb, rc)
        rs = pl.ds(i * rc, rc)
        for lo in range(0, D_MODEL, SLAB):
            cs = slice(lo, lo + SLAB)
            g_a = _sigmoid(ag(rs, lo, SLAB))
            g_b = _sigmoid(ag(rs, D_MODEL + lo, SLAB))
            g_c = _sigmoid(ag(rs, 2 * D_MODEL + lo, SLAB))
            mac_ref[b, pl.ds(r, rc), cs] = g_a * ya[rs, cs] + g_c * yc[rs, cs]
            gb_ref[b, pl.ds(r, rc), cs] = g_b

    _for_each(chunks, g_merge)


def _ssm_kernel(n_ref, h_ref, mac_ref, gb_ref, stmc_ref, stssm_ref,
                wz_ref, wxbc_ref, wdt_ref, cw_ref, cb_ref, dtb_ref, alog_ref, de_ref, ng_ref,
                wso_ref, wo_ref, e_ref, *rest, nb, tt, rc):
    h1_ref, nmc_ref, nssm_ref, xpad, zs, xs_s, bs_s, cs_s, dt_s, acs_s, acst_s, ht, ys = rest[-13:]
    t = pl.program_id(1)
    is_last = t == pl.num_programs(1) - 1
    rows = nb * tt
    cpb = tt // rc
    pmc = _pad_rows(SSM_CONV)
    L = SSD_CHUNK
    seq_mode = tt >= L
    lv = L if seq_mode else tt
    n_chunks = rows // lv

    @pl.when(t == 0)
    def _():
        _load_conv_tail(xpad, pmc, SSM_CONV, stmc_ref)

    def load_state(b):
        for j in range(SSM_INNER // LANES):
            ht[:, j * LANES:(j + 1) * LANES] = stssm_ref[b, j * LANES:(j + 1) * LANES, :].T

    def store_state(b):
        for j in range(SSM_INNER // LANES):
            nssm_ref[b, j * LANES:(j + 1) * LANES, :] = ht[:, j * LANES:(j + 1) * LANES].T

    if seq_mode:
        @pl.when(t == 0)
        def _():
            load_state(0)
    else:
        xs_s[...] = jnp.zeros_like(xs_s)
        bs_s[...] = jnp.zeros_like(bs_s)
        cs_s[...] = jnp.zeros_like(cs_s)
        dt_s[...] = jnp.zeros_like(dt_s)

    def n_tile():
        return n_ref[...].reshape(rows, D_MODEL)

    _store_tile(xpad, pmc, _dot(n_tile(), wxbc_ref[...]), nb, tt)
    dt_all = _softplus(_dot(n_tile(), wdt_ref[...]) + dtb_ref[...])
    if seq_mode:
        dt_s[...] = dt_all
    else:
        for b in range(nb):
            dt_s[b * L:b * L + tt, :] = dt_all[b * tt:(b + 1) * tt, :]

    def conv(i):
        b, r = _chunk_pos(i, cpb, rc)
        dst = i * (rc if seq_mode else L)
        for lb, ls in enumerate(_lane_blocks(SSM_CONV_DIM)):
            acc = None
            for k in range(SSM_CONV):
                term = xpad[b, lb, pl.ds(pmc - (SSM_CONV - 1) + k + r, rc), :] * cw_ref[k:k + 1, ls]
                acc = term if acc is None else acc + term
            xbc = _silu(acc + cb_ref[:, ls])
            if ls.start < SSM_INNER:
                xs_s[pl.ds(dst, rc), ls] = xbc
            elif ls.start < SSM_INNER + SSM_GN:
                bs_s[pl.ds(dst, rc), ls.start - SSM_INNER:ls.stop - SSM_INNER] = xbc
            else:
                cs_s[pl.ds(dst, rc), ls.start - SSM_INNER - SSM_GN:ls.stop - SSM_INNER - SSM_GN] = xbc

    _interleave(rows // rc, conv, _slabs(zs, n_tile, wz_ref))
    _carry_conv_tail(xpad, pmc, SSM_CONV, tt, nmc_ref, is_last)

    a_row = -jnp.exp(alog_ref[...])
    ii = lax.broadcasted_iota(jnp.int32, (L, L), 0)
    jj = lax.broadcasted_iota(jnp.int32, (L, L), 1)
    causal = ii >= jj
    tril = jnp.where(causal, 1.0, 0.0).astype(BF16)
    low_half = (lax.broadcasted_iota(jnp.int32, (1, GROUP_W), 1) % LANES) < SSM_HEAD_DIM
    even_cols = jnp.where(low_half, 1.0, 0.0).astype(BF16)
    odd_cols = jnp.where(low_half, 0.0, 1.0).astype(BF16)

    def pack_terms(x):
        head_lane = lax.broadcasted_iota(jnp.int32, x.shape, 1) < SSM_HEADS
        r = jnp.where(head_lane, x, 0.0)
        packed = None
        for k in range(SPLIT_TERMS):
            p = r.astype(BF16).astype(F32)
            placed = p if k == 0 else pltpu.roll(p, k * SSM_HEADS, 1)
            packed = placed if packed is None else packed + placed
            r = r - p
        return packed.astype(BF16)

    def expand(x):
        return _dot(pack_terms(x), e_ref[...])

    def chunk(c):
        if not seq_mode:
            load_state(c)
        r0 = c * L if isinstance(c, int) else pl.multiple_of(c * L, L)
        rv = c * lv if isinstance(c, int) else pl.multiple_of(c * lv, lv)
        dtc = dt_s[pl.ds(r0, L), :]
        cs3 = _dot(tril, pack_terms(dtc * a_row))
        acs = cs3
        for k in range(1, SPLIT_TERMS):
            acs = acs + pltpu.roll(cs3, LANES - k * SSM_HEADS, 1)
        acs2 = acs * LOG2_E
        acs_s[...] = acs2
        acst_s[...] = acs2.T
        a_last = acs[L - 1:L, :]
        dt_e = expand(dtc)
        eacs_e = expand(jnp.exp(acs))
        dec_e = expand(jnp.exp(a_last - acs))
        elast_e = expand(jnp.broadcast_to(jnp.exp(a_last), (SUBLANES, LANES)))[0:1, :]
        for g in range(SSM_GROUPS):
            gs = slice(g * GROUP_W, (g + 1) * GROUP_W)
            ns = slice(g * SSM_STATE, (g + 1) * SSM_STATE)
            xs_g = xs_s[pl.ds(r0, L), gs]
            xdt = xs_g * dt_e[:, gs]
            xdt_b = xdt.astype(BF16)
            x_even = xdt_b * even_cols
            x_odd = xdt_b * odd_cols
            xdw_b = (xdt * dec_e[:, gs]).astype(BF16)
            c_g = cs_s[pl.ds(r0, L), ns].astype(BF16)
            bt_g = bs_s[pl.ds(r0, L), ns].T.astype(BF16)
            cb = _dot(c_g, bt_g)
            ht_g = ht[:, gs]
            y_off = _dot(c_g, ht_g.astype(BF16)) * eacs_e[:, gs]
            ht[:, gs] = ht_g * elast_e[:, gs] + _dot(bt_g, xdw_b)
            y_cols = []
            for pr in range(GROUP_W // LANES):
                hd = g * (GROUP_W // SSM_HEAD_DIM) + 2 * pr
                ps = slice(pr * LANES, (pr + 1) * LANES)
                ms = []
                for hh in (hd, hd + 1):
                    diff = acs_s[:, hh:hh + 1] - acst_s[hh:hh + 1, :]
                    ms.append((cb * jnp.exp2(jnp.where(causal, diff, -jnp.inf))).astype(BF16))
                y_cols.append(_dot(jnp.concatenate(ms, axis=1),
                                   jnp.concatenate([x_even[:, ps], x_odd[:, ps]], axis=0)))
            y = jnp.concatenate(y_cols, axis=-1) + y_off + de_ref[:, gs] * xs_g
            y = y[0:lv, :] * _silu(zs[pl.ds(rv, lv), gs])
            y = y * lax.rsqrt(jnp.mean(y * y, axis=-1, keepdims=True) + EPS)
            ys[pl.ds(rv, lv), gs] = (y * ng_ref[:, gs]).astype(BF16)
        if not seq_mode:
            store_state(c)

    def finish(b, r, m):
        y_b = _dot(ys[pl.ds(b * tt + r, m), :], wso_ref[...])
        merged = mac_ref[b, pl.ds(r, m), :] + gb_ref[b, pl.ds(r, m), :] * y_b
        h1_ref[b, pl.ds(r, m), :] = h_ref[b, pl.ds(r, m), :] + _dot(merged.astype(BF16), wo_ref[...])

    if seq_mode:
        def chunk_step(c, carry):
            chunk(c)
            finish(0, pl.multiple_of(c * L, L), L)
            return carry

        lax.fori_loop(0, n_chunks, chunk_step, 0)
    else:
        _for_each(n_chunks, chunk)

    if seq_mode:
        @pl.when(is_last)
        def _():
            store_state(0)
    else:
        y_b = _dot(ys[...], wso_ref[...])
        merged = mac_ref[...].reshape(rows, D_MODEL) + gb_ref[...].reshape(rows, D_MODEL) * y_b
        h1 = h_ref[...].reshape(rows, D_MODEL) + _dot(merged.astype(BF16), wo_ref[...])
        h1_ref[...] = h1.reshape(nb, tt, D_MODEL)


def _ffn_kernel(h_ref, p_ref, stff_ref, gffn_ref, wup_ref, fw_ref, fb_ref, wdn_ref, gple_ref,
                wpg_ref, wpp_ref, gfin_ref,
                out_ref, nff_ref,
                xpad, nf_s, lhs, *, nb, tt, rc, final):
    t = pl.program_id(1)
    is_last = t == pl.num_programs(1) - 1
    rows = nb * tt
    cpb = tt // rc
    pff = _pad_rows(FF_KERNEL)
    half_blocks = FF_DIM // LANES
    slab_blocks = MXU_DIM // LANES
    n_slabs = FF_DIM // MXU_DIM

    @pl.when(t == 0)
    def _():
        _load_conv_tail(xpad, pff, FF_KERNEL, stff_ref)

    h1 = h_ref[...].reshape(rows, D_MODEL)
    nf_s[...] = _rms(h1, gffn_ref[...]).astype(BF16)

    def up_slab(j):
        for half in range(2):
            lo = half * FF_DIM + j * MXU_DIM
            _store_tile(xpad, pff, _dot(nf_s[...], wup_ref[:, lo:lo + MXU_DIM]), nb, tt, first_block=lo // LANES)

    def conv_slab(j):
        for i in range(rows // rc):
            b, r = _chunk_pos(i, cpb, rc)
            r2 = i * rc
            for lb in range(j * slab_blocks, (j + 1) * slab_blocks):
                halves = []
                for blk in (lb, half_blocks + lb):
                    cs = slice(blk * LANES, (blk + 1) * LANES)
                    acc = None
                    for k in range(FF_KERNEL):
                        term = xpad[b, blk, pl.ds(pff - (FF_KERNEL - 1) + k + r, rc), :] * fw_ref[k:k + 1, cs]
                        acc = term if acc is None else acc + term
                    halves.append(acc + fb_ref[:, cs])
                lhs[pl.ds(r2, rc), lb * LANES:(lb + 1) * LANES] = (_silu(halves[0]) * halves[1]).astype(BF16)

    up_slab(0)
    for j in range(n_slabs):
        if j + 1 < n_slabs:
            up_slab(j + 1)
        conv_slab(j)
    _carry_conv_tail(xpad, pff, FF_KERNEL, tt, nff_ref, is_last)

    h2 = h1 + _dot(lhs[...], wdn_ref[...])
    gate = _sigmoid(_dot(_rms(h2, gple_ref[...]).astype(BF16), wpg_ref[...]))
    pp = _dot(p_ref[...].reshape(rows, PLE_DIM).astype(BF16), wpp_ref[...])
    h3 = h2 + pp * gate
    if final:
        h3 = _rms(h3, gfin_ref[...])
    out_ref[...] = h3.reshape(nb, tt, D_MODEL)


def _tile_spec(nb, tt, c):
    return pl.BlockSpec((nb, tt, c), lambda b, t: (b, t, 0))


def _layer_tile_spec(layer, nb, tt, c):
    return pl.BlockSpec((None, nb, tt, c), lambda b, t: (layer, b, t, 0))


def _state_spec(nb, r, c):
    return pl.BlockSpec((nb, r, c), lambda b, t: (b, 0, 0))


def _layer_state_spec(layer, nb, r, c):
    return pl.BlockSpec((None, nb, r, c), lambda b, t: (layer, b, 0, 0))


def _const_spec(layer, arr):
    if layer is None:
        return pl.BlockSpec(arr.shape, lambda b, t: (0,) * arr.ndim, pipeline_mode=pl.Buffered(1))
    return pl.BlockSpec((None,) + arr.shape[1:], lambda b, t: (layer,) + (0,) * (arr.ndim - 1),
                        pipeline_mode=pl.Buffered(1))


def _params():
    return pltpu.CompilerParams(dimension_semantics=("arbitrary", "arbitrary"),
                                vmem_limit_bytes=VMEM_LIMIT)


def _conv_buffer(nb, k, tt, c):
    return pltpu.VMEM((nb, c // LANES, _pad_rows(k) + tt, LANES), F32)


def _ac_call(layer, h, st_sc, st_cf, w, nb, tt):
    bsz, tlen, _ = h.shape
    rc = min(tt, 32)
    rows = nb * tt
    consts = [w['g_mix'], w['w_ag'], w['w_cf'], w['sc_conv_w'], w['w_sc_out'],
              w['cf_conv_w'], w['cf_conv_b'], w['cf_ln_g'], w['cf_ln_b'], w['w_cf_out']]
    return pl.pallas_call(
        functools.partial(_ac_kernel, nb=nb, tt=tt, rc=rc),
        grid=(bsz // nb, tlen // tt),
        in_specs=[_tile_spec(nb, tt, D_MODEL), _layer_state_spec(layer, nb, SC_KERNEL - 1, D_MODEL),
                  _layer_state_spec(layer, nb, CF_KERNEL - 1, D_MODEL)] + [_const_spec(layer, a) for a in consts],
        out_specs=[_tile_spec(nb, tt, D_MODEL), _tile_spec(nb, tt, D_MODEL), _tile_spec(nb, tt, D_MODEL),
                   _state_spec(nb, SC_KERNEL - 1, D_MODEL), _state_spec(nb, CF_KERNEL - 1, D_MODEL)],
        out_shape=[jax.ShapeDtypeStruct((bsz, tlen, D_MODEL), BF16),
                   jax.ShapeDtypeStruct((bsz, tlen, D_MODEL), F32),
                   jax.ShapeDtypeStruct((bsz, tlen, D_MODEL), F32),
                   jax.ShapeDtypeStruct((bsz, SC_KERNEL - 1, D_MODEL), F32),
                   jax.ShapeDtypeStruct((bsz, CF_KERNEL - 1, D_MODEL), F32)],
        scratch_shapes=[_conv_buffer(nb, SC_KERNEL, tt, D_MODEL),
                        _conv_buffer(nb, CF_KERNEL, tt, D_MODEL),
                        pltpu.VMEM((OFF_Z // SLAB, rows, SLAB), F32),
                        pltpu.VMEM((rows, 2 * D_MODEL), F32),
                        pltpu.VMEM((rows, D_MODEL), F32),
                        pltpu.VMEM((rows, D_MODEL), BF16),
                        pltpu.VMEM((rows, D_MODEL), BF16),
                        pltpu.VMEM((rows, D_MODEL), F32),
                        pltpu.VMEM((rows, D_MODEL), F32)],
        compiler_params=_params(),
        name="mix_ac",
    )(h, st_sc, st_cf, *consts)


def _ssm_call(layer, n, h, mac, gb, st_mc, st_ssm, new_ssm, w, nb, tt):
    bsz, tlen, _ = h.shape
    rc = min(tt, 32)
    rows = nb * tt
    L = SSD_CHUNK
    prow = rows if tt >= L else nb * L
    consts = [w['w_z'], w['w_xbc'], w['w_dt'], w['ssm_conv_w'], w['ssm_conv_b'], w['dt_bias'], w['a_log'],
              w['d_exp'], w['ssm_norm_g'], w['w_ssm_out'], w['w_o']]
    inputs = [n, h, mac, gb, st_mc, st_ssm, *consts, w['head_expand']]
    in_specs = ([_tile_spec(nb, tt, D_MODEL)] * 4
                + [_layer_state_spec(layer, nb, SSM_CONV - 1, SSM_CONV_DIM),
                   _layer_state_spec(layer, nb, SSM_INNER, SSM_STATE)]
                + [_const_spec(layer, a) for a in consts] + [_const_spec(None, w['head_expand'])])
    aliases = {}
    if new_ssm is not None:
        aliases = {len(inputs): 2}
        inputs.append(new_ssm)
        in_specs.append(pl.BlockSpec(memory_space=pl.ANY))
    return pl.pallas_call(
        functools.partial(_ssm_kernel, nb=nb, tt=tt, rc=rc),
        grid=(bsz // nb, tlen // tt),
        in_specs=in_specs,
        out_specs=[_tile_spec(nb, tt, D_MODEL), _state_spec(nb, SSM_CONV - 1, SSM_CONV_DIM),
                   _layer_state_spec(layer, nb, SSM_INNER, SSM_STATE)],
        out_shape=[jax.ShapeDtypeStruct((bsz, tlen, D_MODEL), F32),
                   jax.ShapeDtypeStruct((bsz, SSM_CONV - 1, SSM_CONV_DIM), F32),
                   jax.ShapeDtypeStruct(st_ssm.shape, F32)],
        input_output_aliases=aliases,
        scratch_shapes=[_conv_buffer(nb, SSM_CONV, tt, SSM_CONV_DIM),
                        pltpu.VMEM((rows, SSM_INNER), F32),
                        pltpu.VMEM((prow, SSM_INNER), F32),
                        pltpu.VMEM((prow, SSM_GN), F32),
                        pltpu.VMEM((prow, SSM_GN), F32),
                        pltpu.VMEM((prow, LANES), F32),
                        pltpu.VMEM((L, LANES), F32),
                        pltpu.VMEM((LANES, L), F32),
                        pltpu.VMEM((SSM_STATE, SSM_INNER), F32),
                        pltpu.VMEM((rows, SSM_INNER), BF16)],
        compiler_params=_params(),
        name="mix_ssm",
    )(*inputs)


def _ffn_call(layer, h1, p, st_ff, w, g_final, nb, tt, final):
    bsz, tlen, _ = h1.shape
    rc = min(tt, 32)
    rows = nb * tt
    consts = [w['g_ffn'], w['w_up'], w['ff_conv_w'], w['ff_conv_b'], w['w_down'], w['g_ple'],
              w['w_ple_gate'], w['w_ple_proj']]
    return pl.pallas_call(
        functools.partial(_ffn_kernel, nb=nb, tt=tt, rc=rc, final=final),
        grid=(bsz // nb, tlen // tt),
        in_specs=[_tile_spec(nb, tt, D_MODEL), _layer_tile_spec(layer, nb, tt, PLE_DIM),
                  _layer_state_spec(layer, nb, FF_KERNEL - 1, 2 * FF_DIM)]
        + [_const_spec(layer, a) for a in consts] + [_const_spec(None, g_final)],
        out_specs=[_tile_spec(nb, tt, D_MODEL), _state_spec(nb, FF_KERNEL - 1, 2 * FF_DIM)],
        out_shape=[jax.ShapeDtypeStruct((bsz, tlen, D_MODEL), F32),
                   jax.ShapeDtypeStruct((bsz, FF_KERNEL - 1, 2 * FF_DIM), F32)],
        scratch_shapes=[_conv_buffer(nb, FF_KERNEL, tt, 2 * FF_DIM),
                        pltpu.VMEM((rows, D_MODEL), BF16),
                        pltpu.VMEM((rows, FF_DIM), BF16)],
        compiler_params=_params(),
        name="ffn_ple",
    )(h1, p, st_ff, *consts, g_final)


def _prep_weights(g_mix, w_in, sc_conv_w, w_sc_out, ssm_conv_w, ssm_conv_b, ssm_dt_bias, ssm_a_log, ssm_d,
                  ssm_norm_g, w_ssm_out, cf_conv_w, cf_conv_b, cf_ln_g, cf_ln_b, w_cf_out, w_o, g_ffn, w_up,
                  ff_conv_w, ff_conv_b, w_down, g_ple, w_ple_gate, w_ple_proj):
    depth = w_in.shape[0]
    row = lambda v: v.reshape(depth, 1, -1)
    lane_pad = lambda v: jnp.pad(v, [(0, 0)] * (v.ndim - 1) + [(0, LANES - v.shape[-1])])
    head_of_channel = jnp.arange(SSM_INNER) // SSM_HEAD_DIM
    term_lane = jnp.arange(LANES)
    return {
        'g_mix': row(g_mix),
        'w_ag': w_in[:, :, 0:OFF_Z].astype(BF16).reshape(depth, D_MODEL, OFF_Z // SLAB, SLAB).transpose(0, 2, 1, 3),
        'w_z': w_in[:, :, OFF_Z:OFF_XBC].astype(BF16),
        'w_xbc': w_in[:, :, OFF_XBC:OFF_DT].astype(BF16),
        'w_dt': lane_pad(w_in[:, :, OFF_DT:OFF_CF]).astype(BF16),
        'w_cf': w_in[:, :, OFF_CF:N_IN].astype(BF16),
        'sc_conv_w': sc_conv_w,
        'w_sc_out': w_sc_out.astype(BF16),
        'ssm_conv_w': ssm_conv_w,
        'ssm_conv_b': row(ssm_conv_b),
        'dt_bias': lane_pad(row(ssm_dt_bias)),
        'a_log': lane_pad(row(ssm_a_log)),
        'd_exp': jnp.repeat(ssm_d, SSM_HEAD_DIM, axis=-1).reshape(depth, 1, SSM_INNER),
        'ssm_norm_g': row(ssm_norm_g),
        'w_ssm_out': w_ssm_out.astype(BF16),
        'cf_conv_w': cf_conv_w,
        'cf_conv_b': row(cf_conv_b),
        'cf_ln_g': row(cf_ln_g),
        'cf_ln_b': row(cf_ln_b),
        'w_cf_out': w_cf_out.astype(BF16),
        'w_o': w_o.astype(BF16),
        'g_ffn': row(g_ffn),
        'w_up': w_up.astype(BF16),
        'ff_conv_w': ff_conv_w,
        'ff_conv_b': row(ff_conv_b),
        'w_down': w_down.astype(BF16),
        'g_ple': row(g_ple),
        'w_ple_gate': w_ple_gate.astype(BF16),
        'w_ple_proj': w_ple_proj.astype(BF16),
        'head_expand': ((term_lane[:, None] % SSM_HEADS == head_of_channel[None, :])
                        & (term_lane[:, None] < SPLIT_TERMS * SSM_HEADS)).astype(BF16),
    }


def _run_trunk(x, p, st_sc, st_mc, st_ssm, st_cf, st_ff, w, g_final, tiles):
    depth, bsz = st_ssm.shape[:2]
    st_ssm = st_ssm.reshape(depth, bsz, SSM_INNER, SSM_STATE)
    h = x
    new_ssm = None
    outs = [[], [], [], []]
    for i in range(depth):
        n, mac, gb, nsc, ncf = _ac_call(i, h, st_sc, st_cf, w, *tiles['ac'])
        h1, nmc, new_ssm = _ssm_call(i, n, h, mac, gb, st_mc, st_ssm, new_ssm, w, *tiles['ssm'])
        h, nff = _ffn_call(i, h1, p, st_ff, w, g_final, *tiles['ffn'], final=(i == depth - 1))
        for lst, s in zip(outs, (nsc, nmc, ncf, nff)):
            lst.append(s)
    nsc, nmc, ncf, nff = [jnp.stack(lst) for lst in outs]
    return h, [nsc, nmc, new_ssm.reshape(depth, bsz, SSM_HEADS, SSM_HEAD_DIM, SSM_STATE), ncf, nff]


def _tiles_for(bsz, tlen):
    if tlen >= 256:
        return {'ac': (1, 256), 'ssm': (1, 256), 'ffn': (1, 256)}
    return {'ac': (min(bsz, 16), tlen), 'ssm': (min(bsz, 4), tlen), 'ffn': (min(bsz, 16), tlen)}


def kernel(x_prompt, x_sample, p_prompt, p_sample, state_short_conv, state_ssm_conv, state_ssm, state_cf_conv, state_ffn_conv, g_mix, w_in, sc_conv_w, w_sc_out, ssm_conv_w, ssm_conv_b, ssm_dt_bias, ssm_a_log, ssm_d, ssm_norm_g, w_ssm_out, cf_conv_w, cf_conv_b, cf_ln_g, cf_ln_b, w_cf_out, w_o, g_ffn, w_up, ff_conv_w, ff_conv_b, w_down, g_ple, w_ple_gate, w_ple_proj, g_final):
    layers = _prep_weights(g_mix, w_in, sc_conv_w, w_sc_out, ssm_conv_w, ssm_conv_b, ssm_dt_bias, ssm_a_log,
                           ssm_d, ssm_norm_g, w_ssm_out, cf_conv_w, cf_conv_b, cf_ln_g, cf_ln_b, w_cf_out, w_o,
                           g_ffn, w_up, ff_conv_w, ff_conv_b, w_down, g_ple, w_ple_gate, w_ple_proj)
    gfin = g_final.reshape(1, D_MODEL)
    bp, tp, _ = x_prompt.shape
    bs, ts, _ = x_sample.shape
    z_sc = jnp.zeros((DEPTH, bp, SC_KERNEL - 1, D_MODEL), F32)
    z_mc = jnp.zeros((DEPTH, bp, SSM_CONV - 1, SSM_CONV_DIM), F32)
    z_ssm = jnp.zeros((DEPTH, bp, SSM_HEADS, SSM_HEAD_DIM, SSM_STATE), F32)
    z_cf = jnp.zeros((DEPTH, bp, CF_KERNEL - 1, D_MODEL), F32)
    z_ff = jnp.zeros((DEPTH, bp, FF_KERNEL - 1, 2 * FF_DIM), F32)
    y_p, sp = _run_trunk(x_prompt, p_prompt, z_sc, z_mc, z_ssm, z_cf, z_ff, layers, gfin, _tiles_for(bp, tp))
    y_s, ss = _run_trunk(x_sample, p_sample, state_short_conv, state_ssm_conv, state_ssm, state_cf_conv,
                         state_ffn_conv, layers, gfin, _tiles_for(bs, ts))
    return (y_p, y_s, sp[0], sp[1], sp[2], sp[3], sp[4], ss[0], ss[1], ss[2], ss[3], ss[4])
```

```python
import functools

import jax
import jax.numpy as jnp
from jax import lax
from jax.experimental import pallas as pl
from jax.experimental.pallas import tpu as pltpu

F32 = jnp.float32
BF16 = jnp.bfloat16

D_MODEL = 1024
DEPTH = 2
PLE_DIM = 256
EPS = 1e-6
SC_KERNEL = 3
SSM_INNER = 2 * D_MODEL
SSM_HEAD_DIM = 64
SSM_HEADS = SSM_INNER // SSM_HEAD_DIM
SSM_GROUPS = 4
SSM_STATE = 128
SSM_CONV = 4
SSM_GN = SSM_GROUPS * SSM_STATE
SSM_CONV_DIM = SSM_INNER + 2 * SSM_GN
CF_KERNEL = 31
FF_DIM = 2816
FF_KERNEL = 3
OFF_SC = 3 * D_MODEL
OFF_Z = OFF_SC + 3 * D_MODEL
OFF_XBC = OFF_Z + SSM_INNER
OFF_DT = OFF_XBC + SSM_CONV_DIM
OFF_CF = OFF_DT + SSM_HEADS
N_IN = OFF_CF + 2 * D_MODEL

LANES = 128
SUBLANES = 8
MXU_DIM = 256
SSD_CHUNK = 128
GROUP_W = SSM_INNER // SSM_GROUPS
SPLIT_TERMS = 3
SLAB = 2 * MXU_DIM
AC_STEPS = 4
VMEM_LIMIT = 56 * 1024 * 1024


def _dot(a, b):
    return jnp.dot(a, b, preferred_element_type=F32)


def _sigmoid(x):
    return 0.5 * jnp.tanh(0.5 * x) + 0.5


def _silu(x):
    h = 0.5 * x
    return h + h * jnp.tanh(h)


def _softplus(x):
    return jnp.maximum(x, 0.0) + jnp.log1p(jnp.exp(-jnp.abs(x)))


def _rms(x, g):
    return x * lax.rsqrt(jnp.mean(x * x, axis=-1, keepdims=True) + EPS) * g


def _pad_rows(k):
    return -(-(k - 1) // SUBLANES) * SUBLANES


def _chunk_pos(i, cpb, rc):
    return i // cpb, (i % cpb) * rc


def _for_each(n, body):
    for i in range(n):
        body(i)


def _slabs(dst, lhs_fn, w_ref):
    thunks = []
    for lo in range(0, w_ref.shape[1], SLAB):
        cs = slice(lo, min(lo + SLAB, w_ref.shape[1]))

        def run(cs=cs):
            dst[:, cs] = _dot(lhs_fn(), w_ref[:, cs])

        thunks.append(run)
    return thunks


def _interleave(n_steps, step, thunks):
    done = 0
    for i in range(n_steps):
        upto = (len(thunks) * (i + 1)) // n_steps
        for th in thunks[done:upto]:
            th()
        done = upto
        step(i)


def _lane_blocks(c):
    return [slice(lb * LANES, (lb + 1) * LANES) for lb in range(c // LANES)]


def _load_conv_tail(xpad, pad, k, st_ref):
    for lb, ls in enumerate(_lane_blocks(st_ref.shape[-1])):
        xpad[:, lb, pad - (k - 1):pad, :] = st_ref[:, :, ls]


def _carry_conv_tail(xpad, pad, k, tt, new_ref, is_last):
    tail = xpad[:, :, pad + tt - (k - 1):pad + tt, :]

    @pl.when(is_last)
    def _():
        for lb, ls in enumerate(_lane_blocks(new_ref.shape[-1])):
            new_ref[:, :, ls] = tail[:, lb]

    xpad[:, :, pad - (k - 1):pad, :] = tail


def _store_tile(xpad, pad, val, nb, tt, first_block=0):
    for lb, ls in enumerate(_lane_blocks(val.shape[-1])):
        xpad[:, first_block + lb, pad:pad + tt, :] = val[:, ls].reshape(nb, tt, LANES)


def _ac_kernel(h_ref, stsc_ref, stcf_ref, gmix_ref, wag_ref, wcf_ref, scw_ref, wsco_ref,
               cfw_ref, cfb_ref, lng_ref, lnb_ref, wcfo_ref,
               n_ref, mac_ref, gb_ref, nsc_ref, ncf_ref,
               xsc, xcf, proj_ag, proj_c, conv_c, lhs_a, lhs_c, ya, yc, *, nb, tt, rc):
    t = pl.program_id(1)
    is_last = t == pl.num_programs(1) - 1
    rows = nb * tt
    cpb = tt // rc
    chunks = rows // rc
    psc, pcf = _pad_rows(SC_KERNEL), _pad_rows(CF_KERNEL)
    n_slabs = wag_ref.shape[0]
    steps = AC_STEPS
    cps, sps = chunks // steps, n_slabs // steps

    @pl.when(t == 0)
    def _():
        _load_conv_tail(xsc, psc, SC_KERNEL, stsc_ref)
        _load_conv_tail(xcf, pcf, CF_KERNEL, stcf_ref)

    h = h_ref[...].reshape(rows, D_MODEL)
    n_ref[...] = _rms(h, gmix_ref[...]).astype(BF16).reshape(nb, tt, D_MODEL)

    def n_tile():
        return n_ref[...].reshape(rows, D_MODEL)

    def ag(rs, col, width=LANES):
        return proj_ag[col // SLAB, rs, col % SLAB:col % SLAB + width]

    proj_c[...] = _dot(n_tile(), wcf_ref[...])

    def c_fill(i):
        b, r = _chunk_pos(i, cpb, rc)
        r2 = i * rc
        for lb, ls in enumerate(_lane_blocks(D_MODEL)):
            c_g = proj_c[pl.ds(r2, rc), D_MODEL + lb * LANES:D_MODEL + (lb + 1) * LANES]
            xcf[b, lb, pl.ds(pcf + r, rc), :] = proj_c[pl.ds(r2, rc), ls] * _sigmoid(c_g)

    _for_each(chunks, c_fill)

    def c_step(s, carry):
        for q in range(sps):
            proj_ag[s * sps + q] = _dot(n_tile(), wag_ref[s * sps + q])
        for lb, ls in enumerate(_lane_blocks(D_MODEL)):
            taps = [jnp.broadcast_to(cfw_ref[k:k + 1, ls], (SUBLANES, LANES)) for k in range(CF_KERNEL)]
            for q in range(cps):
                i = s * cps + q
                b, r = _chunk_pos(i, cpb, rc)
                for rg in range(0, rc, SUBLANES):
                    acc = None
                    for k in range(CF_KERNEL):
                        term = xcf[b, lb, pl.ds(pcf - (CF_KERNEL - 1) + k + r + rg, SUBLANES), :] * taps[k]
                        acc = term if acc is None else acc + term
                    conv_c[pl.ds(pl.multiple_of(i * rc + rg, SUBLANES), SUBLANES), ls] = acc
        for q in range(cps):
            r2 = pl.multiple_of((s * cps + q) * rc, rc)
            v = conv_c[pl.ds(r2, rc), :] + cfb_ref[...]
            mu = jnp.mean(v, axis=-1, keepdims=True)
            vc = v - mu
            var = jnp.mean(vc * vc, axis=-1, keepdims=True)
            ln = vc * lax.rsqrt(var + EPS) * lng_ref[...] + lnb_ref[...]
            lhs_c[pl.ds(r2, rc), :] = _silu(ln).astype(BF16)
        return carry

    lax.fori_loop(0, steps, c_step, 0)
    _carry_conv_tail(xcf, pcf, CF_KERNEL, tt, ncf_ref, is_last)

    def a_fill(i):
        b, r = _chunk_pos(i, cpb, rc)
        rs = pl.ds(i * rc, rc)
        for lb, ls in enumerate(_lane_blocks(D_MODEL)):
            xsc[b, lb, pl.ds(psc + r, rc), :] = (ag(rs, OFF_SC + D_MODEL + lb * LANES)
                                                 * ag(rs, OFF_SC + 2 * D_MODEL + lb * LANES))

    _for_each(chunks, a_fill)

    def a_conv(i):
        b, r = _chunk_pos(i, cpb, rc)
        rs = pl.ds(i * rc, rc)
        for lb, ls in enumerate(_lane_blocks(D_MODEL)):
            u = None
            for k in range(SC_KERNEL):
                term = xsc[b, lb, pl.ds(psc - (SC_KERNEL - 1) + k + r, rc), :] * scw_ref[k:k + 1, ls]
                u = term if u is None else u + term
            lhs_a[rs, ls] = (ag(rs, OFF_SC + lb * LANES) * u).astype(BF16)

    _for_each(chunks, a_conv)
    _carry_conv_tail(xsc, psc, SC_KERNEL, tt, nsc_ref, is_last)
    yc[...] = _dot(lhs_c[...], wcfo_ref[...])
    ya[...] = _dot(lhs_a[...], wsco_ref[...])

    def g_merge(i):
        b, r = _chunk_pos(i, cpb, rc)
        rs = pl.ds(i * rc, rc)
        for lo in range(0, D_MODEL, SLAB):
            cs = slice(lo, lo + SLAB)
            g_a = _sigmoid(ag(rs, lo, SLAB))
            g_b = _sigmoid(ag(rs, D_MODEL + lo, SLAB))
            g_c = _sigmoid(ag(rs, 2 * D_MODEL + lo, SLAB))
            mac_ref[b, pl.ds(r, rc), cs] = g_a * ya[rs, cs] + g_c * yc[rs, cs]
            gb_ref[b, pl.ds(r, rc), cs] = g_b

    _for_each(chunks, g_merge)


def _ssm_kernel(n_ref, h_ref, mac_ref, gb_ref, stmc_ref, stssm_ref,
                wz_ref, wxbc_ref, wdt_ref, cw_ref, cb_ref, dtb_ref, alog_ref, de_ref, ng_ref,
                wso_ref, wo_ref, e_ref, *rest, nb, tt, rc):
    h1_ref, nmc_ref, nssm_ref, xpad, zs, xs_s, bs_s, cs_s, dt_s, acs_s, acst_s, ht, ys = rest[-13:]
    t = pl.program_id(1)
    is_last = t == pl.num_programs(1) - 1
    rows = nb * tt
    cpb = tt // rc
    pmc = _pad_rows(SSM_CONV)
    L = SSD_CHUNK
    seq_mode = tt >= L
    lv = L if seq_mode else tt
    n_chunks = rows // lv

    @pl.when(t == 0)
    def _():
        _load_conv_tail(xpad, pmc, SSM_CONV, stmc_ref)

    def load_state(b):
        for j in range(SSM_INNER // LANES):
            ht[:, j * LANES:(j + 1) * LANES] = stssm_ref[b, j * LANES:(j + 1) * LANES, :].T

    def store_state(b):
        for j in range(SSM_INNER // LANES):
            nssm_ref[b, j * LANES:(j + 1) * LANES, :] = ht[:, j * LANES:(j + 1) * LANES].T

    if seq_mode:
        @pl.when(t == 0)
        def _():
            load_state(0)
    else:
        xs_s[...] = jnp.zeros_like(xs_s)
        bs_s[...] = jnp.zeros_like(bs_s)
        cs_s[...] = jnp.zeros_like(cs_s)
        dt_s[...] = jnp.zeros_like(dt_s)

    def n_tile():
        return n_ref[...].reshape(rows, D_MODEL)

    _store_tile(xpad, pmc, _dot(n_tile(), wxbc_ref[...]), nb, tt)
    dt_all = _softplus(_dot(n_tile(), wdt_ref[...]) + dtb_ref[...])
    if seq_mode:
        dt_s[...] = dt_all
    else:
        for b in range(nb):
            dt_s[b * L:b * L + tt, :] = dt_all[b * tt:(b + 1) * tt, :]

    def conv(i):
        b, r = _chunk_pos(i, cpb, rc)
        dst = i * (rc if seq_mode else L)
        for lb, ls in enumerate(_lane_blocks(SSM_CONV_DIM)):
            acc = None
            for k in range(SSM_CONV):
                term = xpad[b, lb, pl.ds(pmc - (SSM_CONV - 1) + k + r, rc), :] * cw_ref[k:k + 1, ls]
                acc = term if acc is None else acc + term
            xbc = _silu(acc + cb_ref[:, ls])
            if ls.start < SSM_INNER:
                xs_s[pl.ds(dst, rc), ls] = xbc
            elif ls.start < SSM_INNER + SSM_GN:
                bs_s[pl.ds(dst, rc), ls.start - SSM_INNER:ls.stop - SSM_INNER] = xbc
            else:
                cs_s[pl.ds(dst, rc), ls.start - SSM_INNER - SSM_GN:ls.stop - SSM_INNER - SSM_GN] = xbc

    _interleave(rows // rc, conv, _slabs(zs, n_tile, wz_ref))
    _carry_conv_tail(xpad, pmc, SSM_CONV, tt, nmc_ref, is_last)

    a_row = -jnp.exp(alog_ref[...])
    ii = lax.broadcasted_iota(jnp.int32, (L, L), 0)
    jj = lax.broadcasted_iota(jnp.int32, (L, L), 1)
    causal = ii >= jj
    tril = jnp.where(causal, 1.0, 0.0).astype(BF16)
    low_half = (lax.broadcasted_iota(jnp.int32, (L, GROUP_W), 1) % LANES) < SSM_HEAD_DIM

    def pack_terms(x):
        head_lane = lax.broadcasted_iota(jnp.int32, x.shape, 1) < SSM_HEADS
        r = jnp.where(head_lane, x, 0.0)
        packed = None
        for k in range(SPLIT_TERMS):
            p = r.astype(BF16).astype(F32)
            placed = p if k == 0 else pltpu.roll(p, k * SSM_HEADS, 1)
            packed = placed if packed is None else packed + placed
            r = r - p
        return packed.astype(BF16)

    def expand(x):
        return _dot(pack_terms(x), e_ref[...])

    def chunk(c):
        if not seq_mode:
            load_state(c)
        r0 = c * L
        rv = c * lv
        dtc = dt_s[pl.ds(r0, L), :]
        cs3 = _dot(tril, pack_terms(dtc * a_row))
        acs = cs3
        for k in range(1, SPLIT_TERMS):
            acs = acs + pltpu.roll(cs3, LANES - k * SSM_HEADS, 1)
        acs_s[...] = acs
        acst_s[...] = acs.T
        a_last = acs[L - 1:L, :]
        dt_e = expand(dtc)
        eacs_e = expand(jnp.exp(acs))
        dec_e = expand(jnp.exp(a_last - acs))
        elast_e = eacs_e[L - 1:L, :]
        for g in range(SSM_GROUPS):
            gs = slice(g * GROUP_W, (g + 1) * GROUP_W)
            ns = slice(g * SSM_STATE, (g + 1) * SSM_STATE)
            xs_g = xs_s[pl.ds(r0, L), gs]
            xdt = xs_g * dt_e[:, gs]
            x_even = jnp.where(low_half, xdt, 0.0).astype(BF16)
            x_odd = jnp.where(low_half, 0.0, xdt).astype(BF16)
            xdw_b = (xdt * dec_e[:, gs]).astype(BF16)
            c_g = cs_s[pl.ds(r0, L), ns].astype(BF16)
            bt_g = bs_s[pl.ds(r0, L), ns].T.astype(BF16)
            cb = _dot(c_g, bt_g)
            ht_g = ht[:, gs]
            y_off = _dot(c_g, ht_g.astype(BF16)) * eacs_e[:, gs]
            ht[:, gs] = ht_g * elast_e[:, gs] + _dot(bt_g, xdw_b)
            y_cols = []
            for pr in range(GROUP_W // LANES):
                hd = g * (GROUP_W // SSM_HEAD_DIM) + 2 * pr
                ps = slice(pr * LANES, (pr + 1) * LANES)
                ms = []
                for hh in (hd, hd + 1):
                    diff = acs_s[:, hh:hh + 1] - acst_s[hh:hh + 1, :]
                    ms.append((cb * jnp.exp(jnp.where(causal, diff, -jnp.inf))).astype(BF16))
                y_cols.append(_dot(jnp.concatenate(ms, axis=1),
                                   jnp.concatenate([x_even[:, ps], x_odd[:, ps]], axis=0)))
            y = jnp.concatenate(y_cols, axis=-1) + y_off + de_ref[:, gs] * xs_g
            y = y[0:lv, :] * _silu(zs[pl.ds(rv, lv), gs])
            y = y * lax.rsqrt(jnp.mean(y * y, axis=-1, keepdims=True) + EPS)
            ys[pl.ds(rv, lv), gs] = (y * ng_ref[:, gs]).astype(BF16)
        if not seq_mode:
            store_state(c)

    def finish(b, r, m):
        y_b = _dot(ys[pl.ds(b * tt + r, m), :], wso_ref[...])
        merged = mac_ref[b, pl.ds(r, m), :] + gb_ref[b, pl.ds(r, m), :] * y_b
        h1_ref[b, pl.ds(r, m), :] = h_ref[b, pl.ds(r, m), :] + _dot(merged.astype(BF16), wo_ref[...])

    for c in range(n_chunks):
        chunk(c)
        if seq_mode:
            finish(0, c * L, L)

    if seq_mode:
        @pl.when(is_last)
        def _():
            store_state(0)
    else:
        y_b = _dot(ys[...], wso_ref[...])
        merged = mac_ref[...].reshape(rows, D_MODEL) + gb_ref[...].reshape(rows, D_MODEL) * y_b
        h1 = h_ref[...].reshape(rows, D_MODEL) + _dot(merged.astype(BF16), wo_ref[...])
        h1_ref[...] = h1.reshape(nb, tt, D_MODEL)


def _ffn_kernel(h_ref, p_ref, stff_ref, gffn_ref, wup_ref, fw_ref, fb_ref, wdn_ref, gple_ref,
                wpg_ref, wpp_ref, gfin_ref,
                out_ref, nff_ref,
                xpad, nf_s, lhs, *, nb, tt, rc, final):
    t = pl.program_id(1)
    is_last = t == pl.num_programs(1) - 1
    rows = nb * tt
    cpb = tt // rc
    pff = _pad_rows(FF_KERNEL)
    half_blocks = FF_DIM // LANES
    slab_blocks = MXU_DIM // LANES
    n_slabs = FF_DIM // MXU_DIM

    @pl.when(t == 0)
    def _():
        _load_conv_tail(xpad, pff, FF_KERNEL, stff_ref)

    h1 = h_ref[...].reshape(rows, D_MODEL)
    nf_s[...] = _rms(h1, gffn_ref[...]).astype(BF16)

    def up_slab(j):
        for half in range(2):
            lo = half * FF_DIM + j * MXU_DIM
            _store_tile(xpad, pff, _dot(nf_s[...], wup_ref[:, lo:lo + MXU_DIM]), nb, tt, first_block=lo // LANES)

    def conv_slab(j):
        for i in range(rows // rc):
            b, r = _chunk_pos(i, cpb, rc)
            r2 = i * rc
            for lb in range(j * slab_blocks, (j + 1) * slab_blocks):
                halves = []
                for blk in (lb, half_blocks + lb):
                    cs = slice(blk * LANES, (blk + 1) * LANES)
                    acc = None
                    for k in range(FF_KERNEL):
                        term = xpad[b, blk, pl.ds(pff - (FF_KERNEL - 1) + k + r, rc), :] * fw_ref[k:k + 1, cs]
                        acc = term if acc is None else acc + term
                    halves.append(acc + fb_ref[:, cs])
                lhs[pl.ds(r2, rc), lb * LANES:(lb + 1) * LANES] = (_silu(halves[0]) * halves[1]).astype(BF16)

    up_slab(0)
    for j in range(n_slabs):
        if j + 1 < n_slabs:
            up_slab(j + 1)
        conv_slab(j)
    _carry_conv_tail(xpad, pff, FF_KERNEL, tt, nff_ref, is_last)

    h2 = h1 + _dot(lhs[...], wdn_ref[...])
    gate = _sigmoid(_dot(_rms(h2, gple_ref[...]).astype(BF16), wpg_ref[...]))
    pp = _dot(p_ref[...].reshape(rows, PLE_DIM).astype(BF16), wpp_ref[...])
    h3 = h2 + pp * gate
    if final:
        h3 = _rms(h3, gfin_ref[...])
    out_ref[...] = h3.reshape(nb, tt, D_MODEL)


def _tile_spec(nb, tt, c):
    return pl.BlockSpec((nb, tt, c), lambda b, t: (b, t, 0))


def _layer_tile_spec(layer, nb, tt, c):
    return pl.BlockSpec((None, nb, tt, c), lambda b, t: (layer, b, t, 0))


def _state_spec(nb, r, c):
    return pl.BlockSpec((nb, r, c), lambda b, t: (b, 0, 0))


def _layer_state_spec(layer, nb, r, c):
    return pl.BlockSpec((None, nb, r, c), lambda b, t: (layer, b, 0, 0))


def _const_spec(layer, arr):
    if layer is None:
        return pl.BlockSpec(arr.shape, lambda b, t: (0,) * arr.ndim, pipeline_mode=pl.Buffered(1))
    return pl.BlockSpec((None,) + arr.shape[1:], lambda b, t: (layer,) + (0,) * (arr.ndim - 1),
                        pipeline_mode=pl.Buffered(1))


def _params():
    return pltpu.CompilerParams(dimension_semantics=("arbitrary", "arbitrary"),
                                vmem_limit_bytes=VMEM_LIMIT)


def _conv_buffer(nb, k, tt, c):
    return pltpu.VMEM((nb, c // LANES, _pad_rows(k) + tt, LANES), F32)


def _ac_call(layer, h, st_sc, st_cf, w, nb, tt):
    bsz, tlen, _ = h.shape
    rc = min(tt, 32)
    rows = nb * tt
    consts = [w['g_mix'], w['w_ag'], w['w_cf'], w['sc_conv_w'], w['w_sc_out'],
              w['cf_conv_w'], w['cf_conv_b'], w['cf_ln_g'], w['cf_ln_b'], w['w_cf_out']]
    return pl.pallas_call(
        functools.partial(_ac_kernel, nb=nb, tt=tt, rc=rc),
        grid=(bsz // nb, tlen // tt),
        in_specs=[_tile_spec(nb, tt, D_MODEL), _layer_state_spec(layer, nb, SC_KERNEL - 1, D_MODEL),
                  _layer_state_spec(layer, nb, CF_KERNEL - 1, D_MODEL)] + [_const_spec(layer, a) for a in consts],
        out_specs=[_tile_spec(nb, tt, D_MODEL), _tile_spec(nb, tt, D_MODEL), _tile_spec(nb, tt, D_MODEL),
                   _state_spec(nb, SC_KERNEL - 1, D_MODEL), _state_spec(nb, CF_KERNEL - 1, D_MODEL)],
        out_shape=[jax.ShapeDtypeStruct((bsz, tlen, D_MODEL), BF16),
                   jax.ShapeDtypeStruct((bsz, tlen, D_MODEL), F32),
                   jax.ShapeDtypeStruct((bsz, tlen, D_MODEL), F32),
                   jax.ShapeDtypeStruct((bsz, SC_KERNEL - 1, D_MODEL), F32),
                   jax.ShapeDtypeStruct((bsz, CF_KERNEL - 1, D_MODEL), F32)],
        scratch_shapes=[_conv_buffer(nb, SC_KERNEL, tt, D_MODEL),
                        _conv_buffer(nb, CF_KERNEL, tt, D_MODEL),
                        pltpu.VMEM((OFF_Z // SLAB, rows, SLAB), F32),
                        pltpu.VMEM((rows, 2 * D_MODEL), F32),
                        pltpu.VMEM((rows, D_MODEL), F32),
                        pltpu.VMEM((rows, D_MODEL), BF16),
                        pltpu.VMEM((rows, D_MODEL), BF16),
                        pltpu.VMEM((rows, D_MODEL), F32),
                        pltpu.VMEM((rows, D_MODEL), F32)],
        compiler_params=_params(),
        name="mix_ac",
    )(h, st_sc, st_cf, *consts)


def _ssm_call(layer, n, h, mac, gb, st_mc, st_ssm, new_ssm, w, nb, tt):
    bsz, tlen, _ = h.shape
    rc = min(tt, 32)
    rows = nb * tt
    L = SSD_CHUNK
    prow = rows if tt >= L else nb * L
    consts = [w['w_z'], w['w_xbc'], w['w_dt'], w['ssm_conv_w'], w['ssm_conv_b'], w['dt_bias'], w['a_log'],
              w['d_exp'], w['ssm_norm_g'], w['w_ssm_out'], w['w_o']]
    inputs = [n, h, mac, gb, st_mc, st_ssm, *consts, w['head_expand']]
    in_specs = ([_tile_spec(nb, tt, D_MODEL)] * 4
                + [_layer_state_spec(layer, nb, SSM_CONV - 1, SSM_CONV_DIM),
                   _layer_state_spec(layer, nb, SSM_INNER, SSM_STATE)]
                + [_const_spec(layer, a) for a in consts] + [_const_spec(None, w['head_expand'])])
    aliases = {}
    if new_ssm is not None:
        aliases = {len(inputs): 2}
        inputs.append(new_ssm)
        in_specs.append(pl.BlockSpec(memory_space=pl.ANY))
    return pl.pallas_call(
        functools.partial(_ssm_kernel, nb=nb, tt=tt, rc=rc),
        grid=(bsz // nb, tlen // tt),
        in_specs=in_specs,
        out_specs=[_tile_spec(nb, tt, D_MODEL), _state_spec(nb, SSM_CONV - 1, SSM_CONV_DIM),
                   _layer_state_spec(layer, nb, SSM_INNER, SSM_STATE)],
        out_shape=[jax.ShapeDtypeStruct((bsz, tlen, D_MODEL), F32),
                   jax.ShapeDtypeStruct((bsz, SSM_CONV - 1, SSM_CONV_DIM), F32),
                   jax.ShapeDtypeStruct(st_ssm.shape, F32)],
        input_output_aliases=aliases,
        scratch_shapes=[_conv_buffer(nb, SSM_CONV, tt, SSM_CONV_DIM),
                        pltpu.VMEM((rows, SSM_INNER), F32),
                        pltpu.VMEM((prow, SSM_INNER), F32),
                        pltpu.VMEM((prow, SSM_GN), F32),
                        pltpu.VMEM((prow, SSM_GN), F32),
                        pltpu.VMEM((prow, LANES), F32),
                        pltpu.VMEM((L, LANES), F32),
                        pltpu.VMEM((LANES, L), F32),
                        pltpu.VMEM((SSM_STATE, SSM_INNER), F32),
                        pltpu.VMEM((rows, SSM_INNER), BF16)],
        compiler_params=_params(),
        name="mix_ssm",
    )(*inputs)


def _ffn_call(layer, h1, p, st_ff, w, g_final, nb, tt, final):
    bsz, tlen, _ = h1.shape
    rc = min(tt, 32)
    rows = nb * tt
    consts = [w['g_ffn'], w['w_up'], w['ff_conv_w'], w['ff_conv_b'], w['w_down'], w['g_ple'],
              w['w_ple_gate'], w['w_ple_proj']]
    return pl.pallas_call(
        functools.partial(_ffn_kernel, nb=nb, tt=tt, rc=rc, final=final),
        grid=(bsz // nb, tlen // tt),
        in_specs=[_tile_spec(nb, tt, D_MODEL), _layer_tile_spec(layer, nb, tt, PLE_DIM),
                  _layer_state_spec(layer, nb, FF_KERNEL - 1, 2 * FF_DIM)]
        + [_const_spec(layer, a) for a in consts] + [_const_spec(None, g_final)],
        out_specs=[_tile_spec(nb, tt, D_MODEL), _state_spec(nb, FF_KERNEL - 1, 2 * FF_DIM)],
        out_shape=[jax.ShapeDtypeStruct((bsz, tlen, D_MODEL), F32),
                   jax.ShapeDtypeStruct((bsz, FF_KERNEL - 1, 2 * FF_DIM), F32)],
        scratch_shapes=[_conv_buffer(nb, FF_KERNEL, tt, 2 * FF_DIM),
                        pltpu.VMEM((rows, D_MODEL), BF16),
                        pltpu.VMEM((rows, FF_DIM), BF16)],
        compiler_params=_params(),
        name="ffn_ple",
    )(h1, p, st_ff, *consts, g_final)


def _prep_weights(g_mix, w_in, sc_conv_w, w_sc_out, ssm_conv_w, ssm_conv_b, ssm_dt_bias, ssm_a_log, ssm_d,
                  ssm_norm_g, w_ssm_out, cf_conv_w, cf_conv_b, cf_ln_g, cf_ln_b, w_cf_out, w_o, g_ffn, w_up,
                  ff_conv_w, ff_conv_b, w_down, g_ple, w_ple_gate, w_ple_proj):
    depth = w_in.shape[0]
    row = lambda v: v.reshape(depth, 1, -1)
    lane_pad = lambda v: jnp.pad(v, [(0, 0)] * (v.ndim - 1) + [(0, LANES - v.shape[-1])])
    head_of_channel = jnp.arange(SSM_INNER) // SSM_HEAD_DIM
    term_lane = jnp.arange(LANES)
    return {
        'g_mix': row(g_mix),
        'w_ag': w_in[:, :, 0:OFF_Z].astype(BF16).reshape(depth, D_MODEL, OFF_Z // SLAB, SLAB).transpose(0, 2, 1, 3),
        'w_z': w_in[:, :, OFF_Z:OFF_XBC].astype(BF16),
        'w_xbc': w_in[:, :, OFF_XBC:OFF_DT].astype(BF16),
        'w_dt': lane_pad(w_in[:, :, OFF_DT:OFF_CF]).astype(BF16),
        'w_cf': w_in[:, :, OFF_CF:N_IN].astype(BF16),
        'sc_conv_w': sc_conv_w,
        'w_sc_out': w_sc_out.astype(BF16),
        'ssm_conv_w': ssm_conv_w,
        'ssm_conv_b': row(ssm_conv_b),
        'dt_bias': lane_pad(row(ssm_dt_bias)),
        'a_log': lane_pad(row(ssm_a_log)),
        'd_exp': jnp.repeat(ssm_d, SSM_HEAD_DIM, axis=-1).reshape(depth, 1, SSM_INNER),
        'ssm_norm_g': row(ssm_norm_g),
        'w_ssm_out': w_ssm_out.astype(BF16),
        'cf_conv_w': cf_conv_w,
        'cf_conv_b': row(cf_conv_b),
        'cf_ln_g': row(cf_ln_g),
        'cf_ln_b': row(cf_ln_b),
        'w_cf_out': w_cf_out.astype(BF16),
        'w_o': w_o.astype(BF16),
        'g_ffn': row(g_ffn),
        'w_up': w_up.astype(BF16),
        'ff_conv_w': ff_conv_w,
        'ff_conv_b': row(ff_conv_b),
        'w_down': w_down.astype(BF16),
        'g_ple': row(g_ple),
        'w_ple_gate': w_ple_gate.astype(BF16),
        'w_ple_proj': w_ple_proj.astype(BF16),
        'head_expand': ((term_lane[:, None] % SSM_HEADS == head_of_channel[None, :])
                        & (term_lane[:, None] < SPLIT_TERMS * SSM_HEADS)).astype(BF16),
    }


def _run_trunk(x, p, st_sc, st_mc, st_ssm, st_cf, st_ff, w, g_final, tiles):
    depth, bsz = st_ssm.shape[:2]
    st_ssm = st_ssm.reshape(depth, bsz, SSM_INNER, SSM_STATE)
    h = x
    new_ssm = None
    outs = [[], [], [], []]
    for i in range(depth):
        n, mac, gb, nsc, ncf = _ac_call(i, h, st_sc, st_cf, w, *tiles['ac'])
        h1, nmc, new_ssm = _ssm_call(i, n, h, mac, gb, st_mc, st_ssm, new_ssm, w, *tiles['ssm'])
        h, nff = _ffn_call(i, h1, p, st_ff, w, g_final, *tiles['ffn'], final=(i == depth - 1))
        for lst, s in zip(outs, (nsc, nmc, ncf, nff)):
            lst.append(s)
    nsc, nmc, ncf, nff = [jnp.stack(lst) for lst in outs]
    return h, [nsc, nmc, new_ssm.reshape(depth, bsz, SSM_HEADS, SSM_HEAD_DIM, SSM_STATE), ncf, nff]


def _tiles_for(bsz, tlen):
    if tlen >= 256:
        return {'ac': (1, 256), 'ssm': (1, 256), 'ffn': (1, 512 if tlen % 512 == 0 else 256)}
    return {'ac': (min(bsz, 16), tlen), 'ssm': (min(bsz, 4), tlen), 'ffn': (min(bsz, 16), tlen)}


def kernel(x_prompt, x_sample, p_prompt, p_sample, state_short_conv, state_ssm_conv, state_ssm, state_cf_conv, state_ffn_conv, g_mix, w_in, sc_conv_w, w_sc_out, ssm_conv_w, ssm_conv_b, ssm_dt_bias, ssm_a_log, ssm_d, ssm_norm_g, w_ssm_out, cf_conv_w, cf_conv_b, cf_ln_g, cf_ln_b, w_cf_out, w_o, g_ffn, w_up, ff_conv_w, ff_conv_b, w_down, g_ple, w_ple_gate, w_ple_proj, g_final):
    layers = _prep_weights(g_mix, w_in, sc_conv_w, w_sc_out, ssm_conv_w, ssm_conv_b, ssm_dt_bias, ssm_a_log,
                           ssm_d, ssm_norm_g, w_ssm_out, cf_conv_w, cf_conv_b, cf_ln_g, cf_ln_b, w_cf_out, w_o,
                           g_ffn, w_up, ff_conv_w, ff_conv_b, w_down, g_ple, w_ple_gate, w_ple_proj)
    gfin = g_final.reshape(1, D_MODEL)
    bp, tp, _ = x_prompt.shape
    bs, ts, _ = x_sample.shape
    z_sc = jnp.zeros((DEPTH, bp, SC_KERNEL - 1, D_MODEL), F32)
    z_mc = jnp.zeros((DEPTH, bp, SSM_CONV - 1, SSM_CONV_DIM), F32)
    z_ssm = jnp.zeros((DEPTH, bp, SSM_HEADS, SSM_HEAD_DIM, SSM_STATE), F32)
    z_cf = jnp.zeros((DEPTH, bp, CF_KERNEL - 1, D_MODEL), F32)
    z_ff = jnp.zeros((DEPTH, bp, FF_KERNEL - 1, 2 * FF_DIM), F32)
    y_p, sp = _run_trunk(x_prompt, p_prompt, z_sc, z_mc, z_ssm, z_cf, z_ff, layers, gfin, _tiles_for(bp, tp))
    y_s, ss = _run_trunk(x_sample, p_sample, state_short_conv, state_ssm_conv, state_ssm, state_cf_conv,
                         state_ffn_conv, layers, gfin, _tiles_for(bs, ts))
    return (y_p, y_s, sp[0], sp[1], sp[2], sp[3], sp[4], ss[0], ss[1], ss[2], ss[3], ss[4])
```

```python
import functools

import jax
import jax.numpy as jnp
from jax import lax
from jax.experimental import pallas as pl
from jax.experimental.pallas import tpu as pltpu

F32 = jnp.float32
BF16 = jnp.bfloat16

D_MODEL = 1024
DEPTH = 2
PLE_DIM = 256
EPS = 1e-6
SC_KERNEL = 3
SSM_INNER = 2 * D_MODEL
SSM_HEAD_DIM = 64
SSM_HEADS = SSM_INNER // SSM_HEAD_DIM
SSM_GROUPS = 4
SSM_STATE = 128
SSM_CONV = 4
SSM_GN = SSM_GROUPS * SSM_STATE
SSM_CONV_DIM = SSM_INNER + 2 * SSM_GN
CF_KERNEL = 31
FF_DIM = 2816
FF_KERNEL = 3
OFF_SC = 3 * D_MODEL
OFF_Z = OFF_SC + 3 * D_MODEL
OFF_XBC = OFF_Z + SSM_INNER
OFF_DT = OFF_XBC + SSM_CONV_DIM
OFF_CF = OFF_DT + SSM_HEADS
N_IN = OFF_CF + 2 * D_MODEL

LANES = 128
SUBLANES = 8
MXU_DIM = 256
SSD_CHUNK = 128
GROUP_W = SSM_INNER // SSM_GROUPS
SPLIT_TERMS = 3
SLAB = 2 * MXU_DIM
AC_STEPS = 4
VMEM_LIMIT = 56 * 1024 * 1024


def _dot(a, b):
    return jnp.dot(a, b, preferred_element_type=F32)


def _sigmoid(x):
    return 0.5 * jnp.tanh(0.5 * x) + 0.5


def _silu(x):
    h = 0.5 * x
    return h + h * jnp.tanh(h)


def _softplus(x):
    return jnp.maximum(x, 0.0) + jnp.log1p(jnp.exp(-jnp.abs(x)))


def _rms(x, g):
    return x * lax.rsqrt(jnp.mean(x * x, axis=-1, keepdims=True) + EPS) * g


def _pad_rows(k):
    return -(-(k - 1) // SUBLANES) * SUBLANES


def _chunk_pos(i, cpb, rc):
    return i // cpb, (i % cpb) * rc


def _for_each(n, body):
    for i in range(n):
        body(i)


def _lane_blocks(c):
    return [slice(lb * LANES, (lb + 1) * LANES) for lb in range(c // LANES)]


def _load_conv_tail(xpad, pad, k, st_ref):
    for lb, ls in enumerate(_lane_blocks(st_ref.shape[-1])):
        xpad[:, lb, pad - (k - 1):pad, :] = st_ref[:, :, ls]


def _carry_conv_tail(xpad, pad, k, tt, new_ref, is_last):
    tail = xpad[:, :, pad + tt - (k - 1):pad + tt, :]

    @pl.when(is_last)
    def _():
        for lb, ls in enumerate(_lane_blocks(new_ref.shape[-1])):
            new_ref[:, :, ls] = tail[:, lb]

    xpad[:, :, pad - (k - 1):pad, :] = tail


def _store_tile(xpad, pad, val, nb, tt, first_block=0):
    for lb, ls in enumerate(_lane_blocks(val.shape[-1])):
        xpad[:, first_block + lb, pad:pad + tt, :] = val[:, ls].reshape(nb, tt, LANES)


def _ac_kernel(h_ref, stsc_ref, stcf_ref, gmix_ref, wag_ref, wcf_ref, scw_ref, wsco_ref,
               cfw_ref, cfb_ref, lng_ref, lnb_ref, wcfo_ref,
               n_ref, mac_ref, gb_ref, nsc_ref, ncf_ref,
               xsc, xcf, proj_ag, proj_c, conv_c, lhs_a, lhs_c, ya, yc, *, nb, tt, rc):
    t = pl.program_id(1)
    is_last = t == pl.num_programs(1) - 1
    rows = nb * tt
    cpb = tt // rc
    chunks = rows // rc
    psc, pcf = _pad_rows(SC_KERNEL), _pad_rows(CF_KERNEL)
    n_slabs = wag_ref.shape[0]
    steps = AC_STEPS
    cps, sps = chunks // steps, n_slabs // steps

    @pl.when(t == 0)
    def _():
        _load_conv_tail(xsc, psc, SC_KERNEL, stsc_ref)
        _load_conv_tail(xcf, pcf, CF_KERNEL, stcf_ref)

    h = h_ref[...].reshape(rows, D_MODEL)
    n_ref[...] = _rms(h, gmix_ref[...]).astype(BF16).reshape(nb, tt, D_MODEL)

    def n_tile():
        return n_ref[...].reshape(rows, D_MODEL)

    def ag(rs, col, width=LANES):
        return proj_ag[col // SLAB, rs, col % SLAB:col % SLAB + width]

    proj_c[...] = _dot(n_tile(), wcf_ref[...])

    def c_fill(i):
        b, r = _chunk_pos(i, cpb, rc)
        r2 = i * rc
        for lb, ls in enumerate(_lane_blocks(D_MODEL)):
            c_g = proj_c[pl.ds(r2, rc), D_MODEL + lb * LANES:D_MODEL + (lb + 1) * LANES]
            xcf[b, lb, pl.ds(pcf + r, rc), :] = proj_c[pl.ds(r2, rc), ls] * _sigmoid(c_g)

    _for_each(chunks, c_fill)

    def c_step(s, carry):
        for q in range(sps):
            proj_ag[s * sps + q] = _dot(n_tile(), wag_ref[s * sps + q])
        for lb, ls in enumerate(_lane_blocks(D_MODEL)):
            taps = [jnp.broadcast_to(cfw_ref[k:k + 1, ls], (SUBLANES, LANES)) for k in range(CF_KERNEL)]
            for q in range(cps):
                i = s * cps + q
                b, r = _chunk_pos(i, cpb, rc)
                for rg in range(0, rc, SUBLANES):
                    acc = None
                    for k in range(CF_KERNEL):
                        term = xcf[b, lb, pl.ds(pcf - (CF_KERNEL - 1) + k + r + rg, SUBLANES), :] * taps[k]
                        acc = term if acc is None else acc + term
                    conv_c[pl.ds(pl.multiple_of(i * rc + rg, SUBLANES), SUBLANES), ls] = acc
        for q in range(cps):
            r2 = pl.multiple_of((s * cps + q) * rc, rc)
            v = conv_c[pl.ds(r2, rc), :] + cfb_ref[...]
            mu = jnp.mean(v, axis=-1, keepdims=True)
            vc = v - mu
            var = jnp.mean(vc * vc, axis=-1, keepdims=True)
            ln = vc * lax.rsqrt(var + EPS) * lng_ref[...] + lnb_ref[...]
            lhs_c[pl.ds(r2, rc), :] = _silu(ln).astype(BF16)
        return carry

    lax.fori_loop(0, steps, c_step, 0)
    _carry_conv_tail(xcf, pcf, CF_KERNEL, tt, ncf_ref, is_last)

    def a_fill(i):
        b, r = _chunk_pos(i, cpb, rc)
        rs = pl.ds(i * rc, rc)
        for lb, ls in enumerate(_lane_blocks(D_MODEL)):
            xsc[b, lb, pl.ds(psc + r, rc), :] = (ag(rs, OFF_SC + D_MODEL + lb * LANES)
                                                 * ag(rs, OFF_SC + 2 * D_MODEL + lb * LANES))

    _for_each(chunks, a_fill)

    def a_conv(i):
        b, r = _chunk_pos(i, cpb, rc)
        rs = pl.ds(i * rc, rc)
        for lb, ls in enumerate(_lane_blocks(D_MODEL)):
            u = None
            for k in range(SC_KERNEL):
                term = xsc[b, lb, pl.ds(psc - (SC_KERNEL - 1) + k + r, rc), :] * scw_ref[k:k + 1, ls]
                u = term if u is None else u + term
            lhs_a[rs, ls] = (ag(rs, OFF_SC + lb * LANES) * u).astype(BF16)

    _for_each(chunks, a_conv)
    _carry_conv_tail(xsc, psc, SC_KERNEL, tt, nsc_ref, is_last)
    yc[...] = _dot(lhs_c[...], wcfo_ref[...])
    ya[...] = _dot(lhs_a[...], wsco_ref[...])

    def g_merge(i):
        b, r = _chunk_pos(i, cpb, rc)
        rs = pl.ds(i * rc, rc)
        for lo in range(0, D_MODEL, SLAB):
            cs = slice(lo, lo + SLAB)
            g_a = _sigmoid(ag(rs, lo, SLAB))
            g_b = _sigmoid(ag(rs, D_MODEL + lo, SLAB))
            g_c = _sigmoid(ag(rs, 2 * D_MODEL + lo, SLAB))
            mac_ref[b, pl.ds(r, rc), cs] = g_a * ya[rs, cs] + g_c * yc[rs, cs]
            gb_ref[b, pl.ds(r, rc), cs] = g_b

    _for_each(chunks, g_merge)


def _ssm_kernel(n_ref, h_ref, mac_ref, gb_ref, stmc_ref, stssm_ref,
                wz_ref, wxbc_ref, wdt_ref, cw_ref, cb_ref, dtb_ref, alog_ref, de_ref, ng_ref,
                wso_ref, wo_ref, e_ref, state_buf_ref,
                h1_ref, nmc_ref, nssm_ref,
                xpad, zs, xs_s, bs_s, cs_s, dt_s, acs_s, acst_s, ht, ys, *, nb, tt, rc):
    del state_buf_ref
    t = pl.program_id(1)
    is_last = t == pl.num_programs(1) - 1
    rows = nb * tt
    cpb = tt // rc
    pmc = _pad_rows(SSM_CONV)
    L = SSD_CHUNK
    seq_mode = tt >= L
    lv = L if seq_mode else tt
    n_chunks = rows // lv

    @pl.when(t == 0)
    def _():
        _load_conv_tail(xpad, pmc, SSM_CONV, stmc_ref)

    def load_state(b):
        for j in range(SSM_INNER // LANES):
            ht[:, j * LANES:(j + 1) * LANES] = stssm_ref[b, j * LANES:(j + 1) * LANES, :].T

    def store_state(b):
        for j in range(SSM_INNER // LANES):
            nssm_ref[b, j * LANES:(j + 1) * LANES, :] = ht[:, j * LANES:(j + 1) * LANES].T

    if seq_mode:
        @pl.when(t == 0)
        def _():
            load_state(0)
    else:
        xs_s[...] = jnp.zeros_like(xs_s)
        bs_s[...] = jnp.zeros_like(bs_s)
        cs_s[...] = jnp.zeros_like(cs_s)
        dt_s[...] = jnp.zeros_like(dt_s)

    def n_tile():
        return n_ref[...].reshape(rows, D_MODEL)

    _store_tile(xpad, pmc, _dot(n_tile(), wxbc_ref[...]), nb, tt)
    zs[...] = _dot(n_tile(), wz_ref[...])
    dt_all = _softplus(_dot(n_tile(), wdt_ref[...]) + dtb_ref[...])
    if seq_mode:
        dt_s[...] = dt_all
    else:
        for b in range(nb):
            dt_s[b * L:b * L + tt, :] = dt_all[b * tt:(b + 1) * tt, :]

    def conv(i):
        b, r = _chunk_pos(i, cpb, rc)
        dst = i * (rc if seq_mode else L)
        for lb, ls in enumerate(_lane_blocks(SSM_CONV_DIM)):
            acc = None
            for k in range(SSM_CONV):
                term = xpad[b, lb, pl.ds(pmc - (SSM_CONV - 1) + k + r, rc), :] * cw_ref[k:k + 1, ls]
                acc = term if acc is None else acc + term
            xbc = _silu(acc + cb_ref[:, ls])
            if ls.start < SSM_INNER:
                xs_s[pl.ds(dst, rc), ls] = xbc
            elif ls.start < SSM_INNER + SSM_GN:
                bs_s[pl.ds(dst, rc), ls.start - SSM_INNER:ls.stop - SSM_INNER] = xbc
            else:
                cs_s[pl.ds(dst, rc), ls.start - SSM_INNER - SSM_GN:ls.stop - SSM_INNER - SSM_GN] = xbc

    _for_each(rows // rc, conv)
    _carry_conv_tail(xpad, pmc, SSM_CONV, tt, nmc_ref, is_last)

    a_row = -jnp.exp(alog_ref[...])
    ii = lax.broadcasted_iota(jnp.int32, (L, L), 0)
    jj = lax.broadcasted_iota(jnp.int32, (L, L), 1)
    causal = ii >= jj
    tril = jnp.where(causal, 1.0, 0.0).astype(BF16)
    low_half = (lax.broadcasted_iota(jnp.int32, (L, GROUP_W), 1) % LANES) < SSM_HEAD_DIM

    def pack_terms(x):
        head_lane = lax.broadcasted_iota(jnp.int32, x.shape, 1) < SSM_HEADS
        r = jnp.where(head_lane, x, 0.0)
        packed = None
        for k in range(SPLIT_TERMS):
            p = r.astype(BF16).astype(F32)
            placed = p if k == 0 else pltpu.roll(p, k * SSM_HEADS, 1)
            packed = placed if packed is None else packed + placed
            r = r - p
        return packed.astype(BF16)

    def expand(x):
        return _dot(pack_terms(x), e_ref[...])

    def chunk(c):
        if not seq_mode:
            load_state(c)
        r0 = c * L
        rv = c * lv
        dtc = dt_s[pl.ds(r0, L), :]
        cs3 = _dot(tril, pack_terms(dtc * a_row))
        acs = cs3
        for k in range(1, SPLIT_TERMS):
            acs = acs + pltpu.roll(cs3, LANES - k * SSM_HEADS, 1)
        acs_s[...] = acs
        acst_s[...] = acs.T
        a_last = acs[L - 1:L, :]
        dt_e = expand(dtc)
        eacs_e = expand(jnp.exp(acs))
        dec_e = expand(jnp.exp(a_last - acs))
        elast_e = eacs_e[L - 1:L, :]
        for g in range(SSM_GROUPS):
            gs = slice(g * GROUP_W, (g + 1) * GROUP_W)
            ns = slice(g * SSM_STATE, (g + 1) * SSM_STATE)
            xs_g = xs_s[pl.ds(r0, L), gs]
            xdt = xs_g * dt_e[:, gs]
            x_even = jnp.where(low_half, xdt, 0.0).astype(BF16)
            x_odd = jnp.where(low_half, 0.0, xdt).astype(BF16)
            xdw_b = (xdt * dec_e[:, gs]).astype(BF16)
            c_g = cs_s[pl.ds(r0, L), ns].astype(BF16)
            bt_g = bs_s[pl.ds(r0, L), ns].T.astype(BF16)
            cb = _dot(c_g, bt_g)
            ht_g = ht[:, gs]
            y_off = _dot(c_g, ht_g.astype(BF16)) * eacs_e[:, gs]
            ht[:, gs] = ht_g * elast_e[:, gs] + _dot(bt_g, xdw_b)
            y_cols = []
            for pr in range(GROUP_W // LANES):
                hd = g * (GROUP_W // SSM_HEAD_DIM) + 2 * pr
                ps = slice(pr * LANES, (pr + 1) * LANES)
                ms = []
                for hh in (hd, hd + 1):
                    diff = acs_s[:, hh:hh + 1] - acst_s[hh:hh + 1, :]
                    ms.append((cb * jnp.exp(jnp.where(causal, diff, -jnp.inf))).astype(BF16))
                y_cols.append(_dot(jnp.concatenate(ms, axis=1),
                                   jnp.concatenate([x_even[:, ps], x_odd[:, ps]], axis=0)))
            y = jnp.concatenate(y_cols, axis=-1) + y_off + de_ref[:, gs] * xs_g
            y = y[0:lv, :] * _silu(zs[pl.ds(rv, lv), gs])
            y = y * lax.rsqrt(jnp.mean(y * y, axis=-1, keepdims=True) + EPS)
            ys[pl.ds(rv, lv), gs] = (y * ng_ref[:, gs]).astype(BF16)
        if not seq_mode:
            store_state(c)

    def finish(b, r, m):
        y_b = _dot(ys[pl.ds(b * tt + r, m), :], wso_ref[...])
        merged = mac_ref[b, pl.ds(r, m), :] + gb_ref[b, pl.ds(r, m), :] * y_b
        h1_ref[b, pl.ds(r, m), :] = h_ref[b, pl.ds(r, m), :] + _dot(merged.astype(BF16), wo_ref[...])

    for c in range(n_chunks):
        chunk(c)
        if seq_mode:
            finish(0, c * L, L)

    if seq_mode:
        @pl.when(is_last)
        def _():
            store_state(0)
    else:
        y_b = _dot(ys[...], wso_ref[...])
        merged = mac_ref[...].reshape(rows, D_MODEL) + gb_ref[...].reshape(rows, D_MODEL) * y_b
        h1 = h_ref[...].reshape(rows, D_MODEL) + _dot(merged.astype(BF16), wo_ref[...])
        h1_ref[...] = h1.reshape(nb, tt, D_MODEL)


def _ffn_kernel(h_ref, p_ref, stff_ref, gffn_ref, wup_ref, fw_ref, fb_ref, wdn_ref, gple_ref,
                wpg_ref, wpp_ref, gfin_ref,
                out_ref, nff_ref,
                xpad, nf_s, lhs, *, nb, tt, rc, final):
    t = pl.program_id(1)
    is_last = t == pl.num_programs(1) - 1
    rows = nb * tt
    cpb = tt // rc
    pff = _pad_rows(FF_KERNEL)
    half_blocks = FF_DIM // LANES
    slab_blocks = MXU_DIM // LANES
    n_slabs = FF_DIM // MXU_DIM

    @pl.when(t == 0)
    def _():
        _load_conv_tail(xpad, pff, FF_KERNEL, stff_ref)

    h1 = h_ref[...].reshape(rows, D_MODEL)
    nf_s[...] = _rms(h1, gffn_ref[...]).astype(BF16)

    def up_slab(j):
        for half in range(2):
            lo = half * FF_DIM + j * MXU_DIM
            _store_tile(xpad, pff, _dot(nf_s[...], wup_ref[:, lo:lo + MXU_DIM]), nb, tt, first_block=lo // LANES)

    def conv_slab(j):
        for i in range(rows // rc):
            b, r = _chunk_pos(i, cpb, rc)
            r2 = i * rc
            for lb in range(j * slab_blocks, (j + 1) * slab_blocks):
                halves = []
                for blk in (lb, half_blocks + lb):
                    cs = slice(blk * LANES, (blk + 1) * LANES)
                    acc = None
                    for k in range(FF_KERNEL):
                        term = xpad[b, blk, pl.ds(pff - (FF_KERNEL - 1) + k + r, rc), :] * fw_ref[k:k + 1, cs]
                        acc = term if acc is None else acc + term
                    halves.append(acc + fb_ref[:, cs])
                lhs[pl.ds(r2, rc), lb * LANES:(lb + 1) * LANES] = (_silu(halves[0]) * halves[1]).astype(BF16)

    up_slab(0)
    for j in range(n_slabs):
        if j + 1 < n_slabs:
            up_slab(j + 1)
        conv_slab(j)
    _carry_conv_tail(xpad, pff, FF_KERNEL, tt, nff_ref, is_last)

    h2 = h1 + _dot(lhs[...], wdn_ref[...])
    gate = _sigmoid(_dot(_rms(h2, gple_ref[...]).astype(BF16), wpg_ref[...]))
    pp = _dot(p_ref[...].reshape(rows, PLE_DIM).astype(BF16), wpp_ref[...])
    h3 = h2 + pp * gate
    if final:
        h3 = _rms(h3, gfin_ref[...])
    out_ref[...] = h3.reshape(nb, tt, D_MODEL)


def _tile_spec(nb, tt, c):
    return pl.BlockSpec((nb, tt, c), lambda b, t: (b, t, 0))


def _layer_tile_spec(layer, nb, tt, c):
    return pl.BlockSpec((None, nb, tt, c), lambda b, t: (layer, b, t, 0))


def _state_spec(nb, r, c):
    return pl.BlockSpec((nb, r, c), lambda b, t: (b, 0, 0))


def _layer_state_spec(layer, nb, r, c):
    return pl.BlockSpec((None, nb, r, c), lambda b, t: (layer, b, 0, 0))


def _const_spec(layer, arr):
    if layer is None:
        return pl.BlockSpec(arr.shape, lambda b, t: (0,) * arr.ndim, pipeline_mode=pl.Buffered(1))
    return pl.BlockSpec((None,) + arr.shape[1:], lambda b, t: (layer,) + (0,) * (arr.ndim - 1),
                        pipeline_mode=pl.Buffered(1))


def _params():
    return pltpu.CompilerParams(dimension_semantics=("arbitrary", "arbitrary"),
                                vmem_limit_bytes=VMEM_LIMIT)


def _conv_buffer(nb, k, tt, c):
    return pltpu.VMEM((nb, c // LANES, _pad_rows(k) + tt, LANES), F32)


def _ac_call(layer, h, st_sc, st_cf, w, nb, tt):
    bsz, tlen, _ = h.shape
    rc = min(tt, 32)
    rows = nb * tt
    consts = [w['g_mix'], w['w_ag'], w['w_cf'], w['sc_conv_w'], w['w_sc_out'],
              w['cf_conv_w'], w['cf_conv_b'], w['cf_ln_g'], w['cf_ln_b'], w['w_cf_out']]
    return pl.pallas_call(
        functools.partial(_ac_kernel, nb=nb, tt=tt, rc=rc),
        grid=(bsz // nb, tlen // tt),
        in_specs=[_tile_spec(nb, tt, D_MODEL), _layer_state_spec(layer, nb, SC_KERNEL - 1, D_MODEL),
                  _layer_state_spec(layer, nb, CF_KERNEL - 1, D_MODEL)] + [_const_spec(layer, a) for a in consts],
        out_specs=[_tile_spec(nb, tt, D_MODEL), _tile_spec(nb, tt, D_MODEL), _tile_spec(nb, tt, D_MODEL),
                   _state_spec(nb, SC_KERNEL - 1, D_MODEL), _state_spec(nb, CF_KERNEL - 1, D_MODEL)],
        out_shape=[jax.ShapeDtypeStruct((bsz, tlen, D_MODEL), BF16),
                   jax.ShapeDtypeStruct((bsz, tlen, D_MODEL), F32),
                   jax.ShapeDtypeStruct((bsz, tlen, D_MODEL), F32),
                   jax.ShapeDtypeStruct((bsz, SC_KERNEL - 1, D_MODEL), F32),
                   jax.ShapeDtypeStruct((bsz, CF_KERNEL - 1, D_MODEL), F32)],
        scratch_shapes=[_conv_buffer(nb, SC_KERNEL, tt, D_MODEL),
                        _conv_buffer(nb, CF_KERNEL, tt, D_MODEL),
                        pltpu.VMEM((OFF_Z // SLAB, rows, SLAB), F32),
                        pltpu.VMEM((rows, 2 * D_MODEL), F32),
                        pltpu.VMEM((rows, D_MODEL), F32),
                        pltpu.VMEM((rows, D_MODEL), BF16),
                        pltpu.VMEM((rows, D_MODEL), BF16),
                        pltpu.VMEM((rows, D_MODEL), F32),
                        pltpu.VMEM((rows, D_MODEL), F32)],
        compiler_params=_params(),
        name="mix_ac",
    )(h, st_sc, st_cf, *consts)


def _ssm_call(layer, n, h, mac, gb, st_mc, st_ssm, new_ssm, w, nb, tt):
    bsz, tlen, _ = h.shape
    rc = min(tt, 32)
    rows = nb * tt
    L = SSD_CHUNK
    prow = rows if tt >= L else nb * L
    consts = [w['w_z'], w['w_xbc'], w['w_dt'], w['ssm_conv_w'], w['ssm_conv_b'], w['dt_bias'], w['a_log'],
              w['d_exp'], w['ssm_norm_g'], w['w_ssm_out'], w['w_o']]
    inputs = [n, h, mac, gb, st_mc, st_ssm, *consts, w['head_expand'], new_ssm]
    in_specs = ([_tile_spec(nb, tt, D_MODEL)] * 4
                + [_layer_state_spec(layer, nb, SSM_CONV - 1, SSM_CONV_DIM),
                   _layer_state_spec(layer, nb, SSM_INNER, SSM_STATE)]
                + [_const_spec(layer, a) for a in consts] + [_const_spec(None, w['head_expand'])]
                + [pl.BlockSpec(memory_space=pl.ANY)])
    aliases = {len(inputs) - 1: 2}
    return pl.pallas_call(
        functools.partial(_ssm_kernel, nb=nb, tt=tt, rc=rc),
        grid=(bsz // nb, tlen // tt),
        in_specs=in_specs,
        out_specs=[_tile_spec(nb, tt, D_MODEL), _state_spec(nb, SSM_CONV - 1, SSM_CONV_DIM),
                   _layer_state_spec(layer, nb, SSM_INNER, SSM_STATE)],
        out_shape=[jax.ShapeDtypeStruct((bsz, tlen, D_MODEL), F32),
                   jax.ShapeDtypeStruct((bsz, SSM_CONV - 1, SSM_CONV_DIM), F32),
                   jax.ShapeDtypeStruct(st_ssm.shape, F32)],
        input_output_aliases=aliases,
        scratch_shapes=[_conv_buffer(nb, SSM_CONV, tt, SSM_CONV_DIM),
                        pltpu.VMEM((rows, SSM_INNER), F32),
                        pltpu.VMEM((prow, SSM_INNER), F32),
                        pltpu.VMEM((prow, SSM_GN), F32),
                        pltpu.VMEM((prow, SSM_GN), F32),
                        pltpu.VMEM((prow, LANES), F32),
                        pltpu.VMEM((L, LANES), F32),
                        pltpu.VMEM((LANES, L), F32),
                        pltpu.VMEM((SSM_STATE, SSM_INNER), F32),
                        pltpu.VMEM((rows, SSM_INNER), BF16)],
        compiler_params=_params(),
        name="mix_ssm",
    )(*inputs)


def _ffn_call(layer, h1, p, st_ff, w, g_final, nb, tt, final):
    bsz, tlen, _ = h1.shape
    rc = min(tt, 32)
    rows = nb * tt
    consts = [w['g_ffn'], w['w_up'], w['ff_conv_w'], w['ff_conv_b'], w['w_down'], w['g_ple'],
              w['w_ple_gate'], w['w_ple_proj']]
    return pl.pallas_call(
        functools.partial(_ffn_kernel, nb=nb, tt=tt, rc=rc, final=final),
        grid=(bsz // nb, tlen // tt),
        in_specs=[_tile_spec(nb, tt, D_MODEL), _layer_tile_spec(layer, nb, tt, PLE_DIM),
                  _layer_state_spec(layer, nb, FF_KERNEL - 1, 2 * FF_DIM)]
        + [_const_spec(layer, a) for a in consts] + [_const_spec(None, g_final)],
        out_specs=[_tile_spec(nb, tt, D_MODEL), _state_spec(nb, FF_KERNEL - 1, 2 * FF_DIM)],
        out_shape=[jax.ShapeDtypeStruct((bsz, tlen, D_MODEL), F32),
                   jax.ShapeDtypeStruct((bsz, FF_KERNEL - 1, 2 * FF_DIM), F32)],
        scratch_shapes=[_conv_buffer(nb, FF_KERNEL, tt, 2 * FF_DIM),
                        pltpu.VMEM((rows, D_MODEL), BF16),
                        pltpu.VMEM((rows, FF_DIM), BF16)],
        compiler_params=_params(),
        name="ffn_ple",
    )(h1, p, st_ff, *consts, g_final)


def _prep_weights(g_mix, w_in, sc_conv_w, w_sc_out, ssm_conv_w, ssm_conv_b, ssm_dt_bias, ssm_a_log, ssm_d,
                  ssm_norm_g, w_ssm_out, cf_conv_w, cf_conv_b, cf_ln_g, cf_ln_b, w_cf_out, w_o, g_ffn, w_up,
                  ff_conv_w, ff_conv_b, w_down, g_ple, w_ple_gate, w_ple_proj):
    depth = w_in.shape[0]
    row = lambda v: v.reshape(depth, 1, -1)
    lane_pad = lambda v: jnp.pad(v, [(0, 0)] * (v.ndim - 1) + [(0, LANES - v.shape[-1])])
    head_of_channel = jnp.arange(SSM_INNER) // SSM_HEAD_DIM
    term_lane = jnp.arange(LANES)
    return {
        'g_mix': row(g_mix),
        'w_ag': w_in[:, :, 0:OFF_Z].astype(BF16).reshape(depth, D_MODEL, OFF_Z // SLAB, SLAB).transpose(0, 2, 1, 3),
        'w_z': w_in[:, :, OFF_Z:OFF_XBC].astype(BF16),
        'w_xbc': w_in[:, :, OFF_XBC:OFF_DT].astype(BF16),
        'w_dt': lane_pad(w_in[:, :, OFF_DT:OFF_CF]).astype(BF16),
        'w_cf': w_in[:, :, OFF_CF:N_IN].astype(BF16),
        'sc_conv_w': sc_conv_w,
        'w_sc_out': w_sc_out.astype(BF16),
        'ssm_conv_w': ssm_conv_w,
        'ssm_conv_b': row(ssm_conv_b),
        'dt_bias': lane_pad(row(ssm_dt_bias)),
        'a_log': lane_pad(row(ssm_a_log)),
        'd_exp': jnp.repeat(ssm_d, SSM_HEAD_DIM, axis=-1).reshape(depth, 1, SSM_INNER),
        'ssm_norm_g': row(ssm_norm_g),
        'w_ssm_out': w_ssm_out.astype(BF16),
        'cf_conv_w': cf_conv_w,
        'cf_conv_b': row(cf_conv_b),
        'cf_ln_g': row(cf_ln_g),
        'cf_ln_b': row(cf_ln_b),
        'w_cf_out': w_cf_out.astype(BF16),
        'w_o': w_o.astype(BF16),
        'g_ffn': row(g_ffn),
        'w_up': w_up.astype(BF16),
        'ff_conv_w': ff_conv_w,
        'ff_conv_b': row(ff_conv_b),
        'w_down': w_down.astype(BF16),
        'g_ple': row(g_ple),
        'w_ple_gate': w_ple_gate.astype(BF16),
        'w_ple_proj': w_ple_proj.astype(BF16),
        'head_expand': ((term_lane[:, None] % SSM_HEADS == head_of_channel[None, :])
                        & (term_lane[:, None] < SPLIT_TERMS * SSM_HEADS)).astype(BF16),
    }


def _run_trunk(x, p, st_sc, st_mc, st_ssm, st_cf, st_ff, w, g_final, tiles):
    depth, bsz = st_ssm.shape[:2]
    st_ssm = st_ssm.reshape(depth, bsz, SSM_INNER, SSM_STATE)
    h = x
    new_ssm = jnp.zeros(st_ssm.shape, F32)
    outs = [[], [], [], []]
    for i in range(depth):
        n, mac, gb, nsc, ncf = _ac_call(i, h, st_sc, st_cf, w, *tiles['ac'])
        h1, nmc, new_ssm = _ssm_call(i, n, h, mac, gb, st_mc, st_ssm, new_ssm, w, *tiles['ssm'])
        h, nff = _ffn_call(i, h1, p, st_ff, w, g_final, *tiles['ffn'], final=(i == depth - 1))
        for lst, s in zip(outs, (nsc, nmc, ncf, nff)):
            lst.append(s)
    nsc, nmc, ncf, nff = [jnp.stack(lst) for lst in outs]
    return h, [nsc, nmc, new_ssm.reshape(depth, bsz, SSM_HEADS, SSM_HEAD_DIM, SSM_STATE), ncf, nff]


def _tiles_for(bsz, tlen):
    if tlen >= 256:
        return {'ac': (1, 256), 'ssm': (1, 256), 'ffn': (1, 512 if tlen % 512 == 0 else 256)}
    return {'ac': (min(bsz, 16), tlen), 'ssm': (min(bsz, 4), tlen), 'ffn': (min(bsz, 16), tlen)}


def kernel(x_prompt, x_sample, p_prompt, p_sample, state_short_conv, state_ssm_conv, state_ssm, state_cf_conv, state_ffn_conv, g_mix, w_in, sc_conv_w, w_sc_out, ssm_conv_w, ssm_conv_b, ssm_dt_bias, ssm_a_log, ssm_d, ssm_norm_g, w_ssm_out, cf_conv_w, cf_conv_b, cf_ln_g, cf_ln_b, w_cf_out, w_o, g_ffn, w_up, ff_conv_w, ff_conv_b, w_down, g_ple, w_ple_gate, w_ple_proj, g_final):
    layers = _prep_weights(g_mix, w_in, sc_conv_w, w_sc_out, ssm_conv_w, ssm_conv_b, ssm_dt_bias, ssm_a_log,
                           ssm_d, ssm_norm_g, w_ssm_out, cf_conv_w, cf_conv_b, cf_ln_g, cf_ln_b, w_cf_out, w_o,
                           g_ffn, w_up, ff_conv_w, ff_conv_b, w_down, g_ple, w_ple_gate, w_ple_proj)
    gfin = g_final.reshape(1, D_MODEL)
    bp, tp, _ = x_prompt.shape
    bs, ts, _ = x_sample.shape
    z_sc = jnp.zeros((DEPTH, bp, SC_KERNEL - 1, D_MODEL), F32)
    z_mc = jnp.zeros((DEPTH, bp, SSM_CONV - 1, SSM_CONV_DIM), F32)
    z_ssm = jnp.zeros((DEPTH, bp, SSM_HEADS, SSM_HEAD_DIM, SSM_STATE), F32)
    z_cf = jnp.zeros((DEPTH, bp, CF_KERNEL - 1, D_MODEL), F32)
    z_ff = jnp.zeros((DEPTH, bp, FF_KERNEL - 1, 2 * FF_DIM), F32)
    y_p, sp = _run_trunk(x_prompt, p_prompt, z_sc, z_mc, z_ssm, z_cf, z_ff, layers, gfin, _tiles_for(bp, tp))
    y_s, ss = _run_trunk(x_sample, p_sample, state_short_conv, state_ssm_conv, state_ssm, state_cf_conv,
                         state_ffn_conv, layers, gfin, _tiles_for(bs, ts))
    return (y_p, y_s, sp[0], sp[1], sp[2], sp[3], sp[4], ss[0], ss[1], ss[2], ss[3], ss[4])
```

```python
import functools

import jax
import jax.numpy as jnp
from jax import lax
from jax.experimental import pallas as pl
from jax.experimental.pallas import tpu as pltpu

F32 = jnp.float32
BF16 = jnp.bfloat16

D_MODEL = 1024
DEPTH = 2
PLE_DIM = 256
EPS = 1e-6
SC_KERNEL = 3
SSM_INNER = 2 * D_MODEL
SSM_HEAD_DIM = 64
SSM_HEADS = SSM_INNER // SSM_HEAD_DIM
SSM_GROUPS = 4
SSM_STATE = 128
SSM_CONV = 4
SSM_GN = SSM_GROUPS * SSM_STATE
SSM_CONV_DIM = SSM_INNER + 2 * SSM_GN
CF_KERNEL = 31
FF_DIM = 2816
FF_KERNEL = 3
OFF_SC = 3 * D_MODEL
OFF_Z = OFF_SC + 3 * D_MODEL
OFF_XBC = OFF_Z + SSM_INNER
OFF_DT = OFF_XBC + SSM_CONV_DIM
OFF_CF = OFF_DT + SSM_HEADS
N_IN = OFF_CF + 2 * D_MODEL

LANES = 128
SUBLANES = 8
MXU_DIM = 256
SSD_CHUNK = 128
GROUP_W = SSM_INNER // SSM_GROUPS
SPLIT_TERMS = 3
SLAB = 2 * MXU_DIM
AC_STEPS = 4
VMEM_LIMIT = 56 * 1024 * 1024
LOG2_E = 1.4426950408889634


def _dot(a, b):
    return jnp.dot(a, b, preferred_element_type=F32)


def _sigmoid(x):
    return 0.5 * jnp.tanh(0.5 * x) + 0.5


def _silu(x):
    h = 0.5 * x
    return h + h * jnp.tanh(h)


def _softplus(x):
    return jnp.maximum(x, 0.0) + jnp.log1p(jnp.exp(-jnp.abs(x)))


def _rms(x, g):
    return x * lax.rsqrt(jnp.mean(x * x, axis=-1, keepdims=True) + EPS) * g


def _pad_rows(k):
    return -(-(k - 1) // SUBLANES) * SUBLANES


def _chunk_pos(i, cpb, rc):
    return i // cpb, (i % cpb) * rc


def _for_each(n, body):
    for i in range(n):
        body(i)


def _lane_blocks(c):
    return [slice(lb * LANES, (lb + 1) * LANES) for lb in range(c // LANES)]


def _load_conv_tail(xpad, pad, k, st_ref):
    for lb, ls in enumerate(_lane_blocks(st_ref.shape[-1])):
        xpad[:, lb, pad - (k - 1):pad, :] = st_ref[:, :, ls]


def _carry_conv_tail(xpad, pad, k, tt, new_ref, is_last):
    tail = xpad[:, :, pad + tt - (k - 1):pad + tt, :]

    @pl.when(is_last)
    def _():
        for lb, ls in enumerate(_lane_blocks(new_ref.shape[-1])):
            new_ref[:, :, ls] = tail[:, lb]

    xpad[:, :, pad - (k - 1):pad, :] = tail


def _store_tile(xpad, pad, val, nb, tt, first_block=0):
    for lb, ls in enumerate(_lane_blocks(val.shape[-1])):
        xpad[:, first_block + lb, pad:pad + tt, :] = val[:, ls].reshape(nb, tt, LANES)


def _ac_kernel(h_ref, stsc_ref, stcf_ref, gmix_ref, wag_ref, wcf_ref, scw_ref, wsco_ref,
               cfw_ref, cfb_ref, lng_ref, lnb_ref, wcfo_ref,
               n_ref, mac_ref, gb_ref, nsc_ref, ncf_ref,
               xsc, xcf, proj_ag, proj_c, conv_c, lhs_a, lhs_c, ya, yc, *, nb, tt, rc):
    t = pl.program_id(1)
    is_last = t == pl.num_programs(1) - 1
    rows = nb * tt
    cpb = tt // rc
    chunks = rows // rc
    psc, pcf = _pad_rows(SC_KERNEL), _pad_rows(CF_KERNEL)
    n_slabs = wag_ref.shape[0]
    steps = AC_STEPS
    cps, sps = chunks // steps, n_slabs // steps

    @pl.when(t == 0)
    def _():
        _load_conv_tail(xsc, psc, SC_KERNEL, stsc_ref)
        _load_conv_tail(xcf, pcf, CF_KERNEL, stcf_ref)

    h = h_ref[...].reshape(rows, D_MODEL)
    n_ref[...] = _rms(h, gmix_ref[...]).astype(BF16).reshape(nb, tt, D_MODEL)

    def n_tile():
        return n_ref[...].reshape(rows, D_MODEL)

    def ag(rs, col, width=LANES):
        return proj_ag[col // SLAB, rs, col % SLAB:col % SLAB + width]

    proj_c[...] = _dot(n_tile(), wcf_ref[...])

    def c_fill(i):
        b, r = _chunk_pos(i, cpb, rc)
        r2 = i * rc
        for lb, ls in enumerate(_lane_blocks(D_MODEL)):
            c_g = proj_c[pl.ds(r2, rc), D_MODEL + lb * LANES:D_MODEL + (lb + 1) * LANES]
            xcf[b, lb, pl.ds(pcf + r, rc), :] = proj_c[pl.ds(r2, rc), ls] * _sigmoid(c_g)

    _for_each(chunks, c_fill)

    def c_step(s, carry):
        for q in range(sps):
            proj_ag[s * sps + q] = _dot(n_tile(), wag_ref[s * sps + q])
        for lb, ls in enumerate(_lane_blocks(D_MODEL)):
            taps = [jnp.broadcast_to(cfw_ref[k:k + 1, ls], (SUBLANES, LANES)) for k in range(CF_KERNEL)]
            for q in range(cps):
                i = s * cps + q
                b, r = _chunk_pos(i, cpb, rc)
                for rg in range(0, rc, SUBLANES):
                    acc = None
                    for k in range(CF_KERNEL):
                        term = xcf[b, lb, pl.ds(pcf - (CF_KERNEL - 1) + k + r + rg, SUBLANES), :] * taps[k]
                        acc = term if acc is None else acc + term
                    conv_c[pl.ds(pl.multiple_of(i * rc + rg, SUBLANES), SUBLANES), ls] = acc
        for q in range(cps):
            r2 = pl.multiple_of((s * cps + q) * rc, rc)
            v = conv_c[pl.ds(r2, rc), :] + cfb_ref[...]
            mu = jnp.mean(v, axis=-1, keepdims=True)
            vc = v - mu
            var = jnp.mean(vc * vc, axis=-1, keepdims=True)
            ln = vc * lax.rsqrt(var + EPS) * lng_ref[...] + lnb_ref[...]
            lhs_c[pl.ds(r2, rc), :] = _silu(ln).astype(BF16)
        return carry

    lax.fori_loop(0, steps, c_step, 0)
    _carry_conv_tail(xcf, pcf, CF_KERNEL, tt, ncf_ref, is_last)

    def a_fill(i):
        b, r = _chunk_pos(i, cpb, rc)
        rs = pl.ds(i * rc, rc)
        for lb, ls in enumerate(_lane_blocks(D_MODEL)):
            xsc[b, lb, pl.ds(psc + r, rc), :] = (ag(rs, OFF_SC + D_MODEL + lb * LANES)
                                                 * ag(rs, OFF_SC + 2 * D_MODEL + lb * LANES))

    _for_each(chunks, a_fill)

    def a_conv(i):
        b, r = _chunk_pos(i, cpb, rc)
        rs = pl.ds(i * rc, rc)
        for lb, ls in enumerate(_lane_blocks(D_MODEL)):
            u = None
            for k in range(SC_KERNEL):
                term = xsc[b, lb, pl.ds(psc - (SC_KERNEL - 1) + k + r, rc), :] * scw_ref[k:k + 1, ls]
                u = term if u is None else u + term
            lhs_a[rs, ls] = (ag(rs, OFF_SC + lb * LANES) * u).astype(BF16)

    _for_each(chunks, a_conv)
    _carry_conv_tail(xsc, psc, SC_KERNEL, tt, nsc_ref, is_last)
    yc[...] = _dot(lhs_c[...], wcfo_ref[...])
    ya[...] = _dot(lhs_a[...], wsco_ref[...])

    def g_merge(i):
        b, r = _chunk_pos(i, cpb, rc)
        rs = pl.ds(i * rc, rc)
        for lo in range(0, D_MODEL, SLAB):
            cs = slice(lo, lo + SLAB)
            g_a = _sigmoid(ag(rs, lo, SLAB))
            g_b = _sigmoid(ag(rs, D_MODEL + lo, SLAB))
            g_c = _sigmoid(ag(rs, 2 * D_MODEL + lo, SLAB))
            mac_ref[b, pl.ds(r, rc), cs] = g_a * ya[rs, cs] + g_c * yc[rs, cs]
            gb_ref[b, pl.ds(r, rc), cs] = g_b

    _for_each(chunks, g_merge)


def _ssm_kernel(n_ref, h_ref, mac_ref, gb_ref, stmc_ref, stssm_ref,
                wz_ref, wxbc_ref, wdt_ref, cw_ref, cb_ref, dtb_ref, alog_ref, de_ref, ng_ref,
                wso_ref, wo_ref, e_ref, state_buf_ref,
                h1_ref, nmc_ref, nssm_ref,
                xpad, zs, xs_s, bs_s, cs_s, dt_s, acs_s, acst_s, ht, ys, *, nb, tt, rc):
    del state_buf_ref
    t = pl.program_id(1)
    is_last = t == pl.num_programs(1) - 1
    rows = nb * tt
    cpb = tt // rc
    pmc = _pad_rows(SSM_CONV)
    L = SSD_CHUNK
    seq_mode = tt >= L
    lv = L if seq_mode else tt
    n_chunks = rows // lv

    @pl.when(t == 0)
    def _():
        _load_conv_tail(xpad, pmc, SSM_CONV, stmc_ref)

    def load_state(b):
        for j in range(SSM_INNER // LANES):
            ht[:, j * LANES:(j + 1) * LANES] = stssm_ref[b, j * LANES:(j + 1) * LANES, :].T

    def store_state(b):
        for j in range(SSM_INNER // LANES):
            nssm_ref[b, j * LANES:(j + 1) * LANES, :] = ht[:, j * LANES:(j + 1) * LANES].T

    if seq_mode:
        @pl.when(t == 0)
        def _():
            load_state(0)
    else:
        xs_s[...] = jnp.zeros_like(xs_s)
        bs_s[...] = jnp.zeros_like(bs_s)
        cs_s[...] = jnp.zeros_like(cs_s)
        dt_s[...] = jnp.zeros_like(dt_s)

    def n_tile():
        return n_ref[...].reshape(rows, D_MODEL)

    _store_tile(xpad, pmc, _dot(n_tile(), wxbc_ref[...]), nb, tt)
    zs[...] = _dot(n_tile(), wz_ref[...])
    dt_all = _softplus(_dot(n_tile(), wdt_ref[...]) + dtb_ref[...])
    if seq_mode:
        dt_s[...] = dt_all
    else:
        for b in range(nb):
            dt_s[b * L:b * L + tt, :] = dt_all[b * tt:(b + 1) * tt, :]

    def conv(i):
        b, r = _chunk_pos(i, cpb, rc)
        dst = i * (rc if seq_mode else L)
        for lb, ls in enumerate(_lane_blocks(SSM_CONV_DIM)):
            acc = None
            for k in range(SSM_CONV):
                term = xpad[b, lb, pl.ds(pmc - (SSM_CONV - 1) + k + r, rc), :] * cw_ref[k:k + 1, ls]
                acc = term if acc is None else acc + term
            xbc = _silu(acc + cb_ref[:, ls])
            if ls.start < SSM_INNER:
                xs_s[pl.ds(dst, rc), ls] = xbc
            elif ls.start < SSM_INNER + SSM_GN:
                bs_s[pl.ds(dst, rc), ls.start - SSM_INNER:ls.stop - SSM_INNER] = xbc
            else:
                cs_s[pl.ds(dst, rc), ls.start - SSM_INNER - SSM_GN:ls.stop - SSM_INNER - SSM_GN] = xbc

    _for_each(rows // rc, conv)
    _carry_conv_tail(xpad, pmc, SSM_CONV, tt, nmc_ref, is_last)

    a_row = -jnp.exp(alog_ref[...])
    ii = lax.broadcasted_iota(jnp.int32, (L, L), 0)
    jj = lax.broadcasted_iota(jnp.int32, (L, L), 1)
    causal = ii >= jj
    tril = jnp.where(causal, 1.0, 0.0).astype(BF16)
    low_half = (lax.broadcasted_iota(jnp.int32, (1, GROUP_W), 1) % LANES) < SSM_HEAD_DIM
    even_cols = jnp.where(low_half, 1.0, 0.0).astype(BF16)
    odd_cols = jnp.where(low_half, 0.0, 1.0).astype(BF16)

    def pack_terms(x):
        head_lane = lax.broadcasted_iota(jnp.int32, x.shape, 1) < SSM_HEADS
        r = jnp.where(head_lane, x, 0.0)
        packed = None
        for k in range(SPLIT_TERMS):
            p = r.astype(BF16).astype(F32)
            placed = p if k == 0 else pltpu.roll(p, k * SSM_HEADS, 1)
            packed = placed if packed is None else packed + placed
            r = r - p
        return packed.astype(BF16)

    def expand(x):
        return _dot(pack_terms(x), e_ref[...])

    def chunk(c):
        if not seq_mode:
            load_state(c)
        r0 = c * L
        rv = c * lv
        dtc = dt_s[pl.ds(r0, L), :]
        cs3 = _dot(tril, pack_terms(dtc * a_row))
        acs = cs3
        for k in range(1, SPLIT_TERMS):
            acs = acs + pltpu.roll(cs3, LANES - k * SSM_HEADS, 1)
        acs2 = acs * LOG2_E
        acs_s[...] = acs2
        acst_s[...] = acs2.T
        a_last = acs[L - 1:L, :]
        dt_e = expand(dtc)
        eacs_e = expand(jnp.exp(acs))
        dec_e = expand(jnp.exp(a_last - acs))
        elast_e = eacs_e[L - 1:L, :]
        for g in range(SSM_GROUPS):
            gs = slice(g * GROUP_W, (g + 1) * GROUP_W)
            ns = slice(g * SSM_STATE, (g + 1) * SSM_STATE)
            xs_g = xs_s[pl.ds(r0, L), gs]
            xdt = xs_g * dt_e[:, gs]
            xdt_b = xdt.astype(BF16)
            x_even = xdt_b * even_cols
            x_odd = xdt_b * odd_cols
            xdw_b = (xdt * dec_e[:, gs]).astype(BF16)
            c_g = cs_s[pl.ds(r0, L), ns].astype(BF16)
            bt_g = bs_s[pl.ds(r0, L), ns].T.astype(BF16)
            cb = _dot(c_g, bt_g)
            ht_g = ht[:, gs]
            y_off = _dot(c_g, ht_g.astype(BF16)) * eacs_e[:, gs]
            ht[:, gs] = ht_g * elast_e[:, gs] + _dot(bt_g, xdw_b)
            y_cols = []
            for pr in range(GROUP_W // LANES):
                hd = g * (GROUP_W // SSM_HEAD_DIM) + 2 * pr
                ps = slice(pr * LANES, (pr + 1) * LANES)
                ms = []
                for hh in (hd, hd + 1):
                    diff = acs_s[:, hh:hh + 1] - acst_s[hh:hh + 1, :]
                    ms.append((cb * jnp.exp2(jnp.where(causal, diff, -jnp.inf))).astype(BF16))
                y_cols.append(_dot(jnp.concatenate(ms, axis=1),
                                   jnp.concatenate([x_even[:, ps], x_odd[:, ps]], axis=0)))
            y = jnp.concatenate(y_cols, axis=-1) + y_off + de_ref[:, gs] * xs_g
            y = y[0:lv, :] * _silu(zs[pl.ds(rv, lv), gs])
            y = y * lax.rsqrt(jnp.mean(y * y, axis=-1, keepdims=True) + EPS)
            ys[pl.ds(rv, lv), gs] = (y * ng_ref[:, gs]).astype(BF16)
        if not seq_mode:
            store_state(c)

    def finish(b, r, m):
        y_b = _dot(ys[pl.ds(b * tt + r, m), :], wso_ref[...])
        merged = mac_ref[b, pl.ds(r, m), :] + gb_ref[b, pl.ds(r, m), :] * y_b
        h1_ref[b, pl.ds(r, m), :] = h_ref[b, pl.ds(r, m), :] + _dot(merged.astype(BF16), wo_ref[...])

    for c in range(n_chunks):
        chunk(c)
        if seq_mode:
            finish(0, c * L, L)

    if seq_mode:
        @pl.when(is_last)
        def _():
            store_state(0)
    else:
        y_b = _dot(ys[...], wso_ref[...])
        merged = mac_ref[...].reshape(rows, D_MODEL) + gb_ref[...].reshape(rows, D_MODEL) * y_b
        h1 = h_ref[...].reshape(rows, D_MODEL) + _dot(merged.astype(BF16), wo_ref[...])
        h1_ref[...] = h1.reshape(nb, tt, D_MODEL)


def _ffn_kernel(h_ref, p_ref, stff_ref, gffn_ref, wup_ref, fw_ref, fb_ref, wdn_ref, gple_ref,
                wpg_ref, wpp_ref, gfin_ref,
                out_ref, nff_ref,
                xpad, nf_s, lhs, *, nb, tt, rc, final):
    t = pl.program_id(1)
    is_last = t == pl.num_programs(1) - 1
    rows = nb * tt
    cpb = tt // rc
    pff = _pad_rows(FF_KERNEL)
    half_blocks = FF_DIM // LANES
    slab_blocks = MXU_DIM // LANES
    n_slabs = FF_DIM // MXU_DIM

    @pl.when(t == 0)
    def _():
        _load_conv_tail(xpad, pff, FF_KERNEL, stff_ref)

    h1 = h_ref[...].reshape(rows, D_MODEL)
    nf_s[...] = _rms(h1, gffn_ref[...]).astype(BF16)

    def up_slab(j):
        for half in range(2):
            lo = half * FF_DIM + j * MXU_DIM
            _store_tile(xpad, pff, _dot(nf_s[...], wup_ref[:, lo:lo + MXU_DIM]), nb, tt, first_block=lo // LANES)

    def conv_slab(j):
        for i in range(rows // rc):
            b, r = _chunk_pos(i, cpb, rc)
            r2 = i * rc
            for lb in range(j * slab_blocks, (j + 1) * slab_blocks):
                halves = []
                for blk in (lb, half_blocks + lb):
                    cs = slice(blk * LANES, (blk + 1) * LANES)
                    acc = None
                    for k in range(FF_KERNEL):
                        term = xpad[b, blk, pl.ds(pff - (FF_KERNEL - 1) + k + r, rc), :] * fw_ref[k:k + 1, cs]
                        acc = term if acc is None else acc + term
                    halves.append(acc + fb_ref[:, cs])
                lhs[pl.ds(r2, rc), lb * LANES:(lb + 1) * LANES] = (_silu(halves[0]) * halves[1]).astype(BF16)

    up_slab(0)
    for j in range(n_slabs):
        if j + 1 < n_slabs:
            up_slab(j + 1)
        conv_slab(j)
    _carry_conv_tail(xpad, pff, FF_KERNEL, tt, nff_ref, is_last)

    h2 = h1 + _dot(lhs[...], wdn_ref[...])
    gate = _sigmoid(_dot(_rms(h2, gple_ref[...]).astype(BF16), wpg_ref[...]))
    pp = _dot(p_ref[...].reshape(rows, PLE_DIM).astype(BF16), wpp_ref[...])
    h3 = h2 + pp * gate
    if final:
        h3 = _rms(h3, gfin_ref[...])
    out_ref[...] = h3.reshape(nb, tt, D_MODEL)


def _tile_spec(nb, tt, c):
    return pl.BlockSpec((nb, tt, c), lambda b, t: (b, t, 0))


def _layer_tile_spec(layer, nb, tt, c):
    return pl.BlockSpec((None, nb, tt, c), lambda b, t: (layer, b, t, 0))


def _state_spec(nb, r, c):
    return pl.BlockSpec((nb, r, c), lambda b, t: (b, 0, 0))


def _layer_state_spec(layer, nb, r, c):
    return pl.BlockSpec((None, nb, r, c), lambda b, t: (layer, b, 0, 0))


def _const_spec(layer, arr):
    if layer is None:
        return pl.BlockSpec(arr.shape, lambda b, t: (0,) * arr.ndim, pipeline_mode=pl.Buffered(1))
    return pl.BlockSpec((None,) + arr.shape[1:], lambda b, t: (layer,) + (0,) * (arr.ndim - 1),
                        pipeline_mode=pl.Buffered(1))


def _params():
    return pltpu.CompilerParams(dimension_semantics=("arbitrary", "arbitrary"),
                                vmem_limit_bytes=VMEM_LIMIT)


def _conv_buffer(nb, k, tt, c):
    return pltpu.VMEM((nb, c // LANES, _pad_rows(k) + tt, LANES), F32)


def _ac_call(layer, h, st_sc, st_cf, w, nb, tt):
    bsz, tlen, _ = h.shape
    rc = min(tt, 32)
    rows = nb * tt
    consts = [w['g_mix'], w['w_ag'], w['w_cf'], w['sc_conv_w'], w['w_sc_out'],
              w['cf_conv_w'], w['cf_conv_b'], w['cf_ln_g'], w['cf_ln_b'], w['w_cf_out']]
    return pl.pallas_call(
        functools.partial(_ac_kernel, nb=nb, tt=tt, rc=rc),
        grid=(bsz // nb, tlen // tt),
        in_specs=[_tile_spec(nb, tt, D_MODEL), _layer_state_spec(layer, nb, SC_KERNEL - 1, D_MODEL),
                  _layer_state_spec(layer, nb, CF_KERNEL - 1, D_MODEL)] + [_const_spec(layer, a) for a in consts],
        out_specs=[_tile_spec(nb, tt, D_MODEL), _tile_spec(nb, tt, D_MODEL), _tile_spec(nb, tt, D_MODEL),
                   _state_spec(nb, SC_KERNEL - 1, D_MODEL), _state_spec(nb, CF_KERNEL - 1, D_MODEL)],
        out_shape=[jax.ShapeDtypeStruct((bsz, tlen, D_MODEL), BF16),
                   jax.ShapeDtypeStruct((bsz, tlen, D_MODEL), F32),
                   jax.ShapeDtypeStruct((bsz, tlen, D_MODEL), F32),
                   jax.ShapeDtypeStruct((bsz, SC_KERNEL - 1, D_MODEL), F32),
                   jax.ShapeDtypeStruct((bsz, CF_KERNEL - 1, D_MODEL), F32)],
        scratch_shapes=[_conv_buffer(nb, SC_KERNEL, tt, D_MODEL),
                        _conv_buffer(nb, CF_KERNEL, tt, D_MODEL),
                        pltpu.VMEM((OFF_Z // SLAB, rows, SLAB), F32),
                        pltpu.VMEM((rows, 2 * D_MODEL), F32),
                        pltpu.VMEM((rows, D_MODEL), F32),
                        pltpu.VMEM((rows, D_MODEL), BF16),
                        pltpu.VMEM((rows, D_MODEL), BF16),
                        pltpu.VMEM((rows, D_MODEL), F32),
                        pltpu.VMEM((rows, D_MODEL), F32)],
        compiler_params=_params(),
        name="mix_ac",
    )(h, st_sc, st_cf, *consts)


def _ssm_call(layer, n, h, mac, gb, st_mc, st_ssm, new_ssm, w, nb, tt):
    bsz, tlen, _ = h.shape
    rc = min(tt, 32)
    rows = nb * tt
    L = SSD_CHUNK
    prow = rows if tt >= L else nb * L
    consts = [w['w_z'], w['w_xbc'], w['w_dt'], w['ssm_conv_w'], w['ssm_conv_b'], w['dt_bias'], w['a_log'],
              w['d_exp'], w['ssm_norm_g'], w['w_ssm_out'], w['w_o']]
    inputs = [n, h, mac, gb, st_mc, st_ssm, *consts, w['head_expand'], new_ssm]
    in_specs = ([_tile_spec(nb, tt, D_MODEL)] * 4
                + [_layer_state_spec(layer, nb, SSM_CONV - 1, SSM_CONV_DIM),
                   _layer_state_spec(layer, nb, SSM_INNER, SSM_STATE)]
                + [_const_spec(layer, a) for a in consts] + [_const_spec(None, w['head_expand'])]
                + [pl.BlockSpec(memory_space=pl.ANY)])
    aliases = {len(inputs) - 1: 2}
    return pl.pallas_call(
        functools.partial(_ssm_kernel, nb=nb, tt=tt, rc=rc),
        grid=(bsz // nb, tlen // tt),
        in_specs=in_specs,
        out_specs=[_tile_spec(nb, tt, D_MODEL), _state_spec(nb, SSM_CONV - 1, SSM_CONV_DIM),
                   _layer_state_spec(layer, nb, SSM_INNER, SSM_STATE)],
        out_shape=[jax.ShapeDtypeStruct((bsz, tlen, D_MODEL), F32),
                   jax.ShapeDtypeStruct((bsz, SSM_CONV - 1, SSM_CONV_DIM), F32),
                   jax.ShapeDtypeStruct(st_ssm.shape, F32)],
        input_output_aliases=aliases,
        scratch_shapes=[_conv_buffer(nb, SSM_CONV, tt, SSM_CONV_DIM),
                        pltpu.VMEM((rows, SSM_INNER), F32),
                        pltpu.VMEM((prow, SSM_INNER), F32),
                        pltpu.VMEM((prow, SSM_GN), F32),
                        pltpu.VMEM((prow, SSM_GN), F32),
                        pltpu.VMEM((prow, LANES), F32),
                        pltpu.VMEM((L, LANES), F32),
                        pltpu.VMEM((LANES, L), F32),
                        pltpu.VMEM((SSM_STATE, SSM_INNER), F32),
                        pltpu.VMEM((rows, SSM_INNER), BF16)],
        compiler_params=_params(),
        name="mix_ssm",
    )(*inputs)


def _ffn_call(layer, h1, p, st_ff, w, g_final, nb, tt, final):
    bsz, tlen, _ = h1.shape
    rc = min(tt, 32)
    rows = nb * tt
    consts = [w['g_ffn'], w['w_up'], w['ff_conv_w'], w['ff_conv_b'], w['w_down'], w['g_ple'],
              w['w_ple_gate'], w['w_ple_proj']]
    return pl.pallas_call(
        functools.partial(_ffn_kernel, nb=nb, tt=tt, rc=rc, final=final),
        grid=(bsz // nb, tlen // tt),
        in_specs=[_tile_spec(nb, tt, D_MODEL), _layer_tile_spec(layer, nb, tt, PLE_DIM),
                  _layer_state_spec(layer, nb, FF_KERNEL - 1, 2 * FF_DIM)]
        + [_const_spec(layer, a) for a in consts] + [_const_spec(None, g_final)],
        out_specs=[_tile_spec(nb, tt, D_MODEL), _state_spec(nb, FF_KERNEL - 1, 2 * FF_DIM)],
        out_shape=[jax.ShapeDtypeStruct((bsz, tlen, D_MODEL), F32),
                   jax.ShapeDtypeStruct((bsz, FF_KERNEL - 1, 2 * FF_DIM), F32)],
        scratch_shapes=[_conv_buffer(nb, FF_KERNEL, tt, 2 * FF_DIM),
                        pltpu.VMEM((rows, D_MODEL), BF16),
                        pltpu.VMEM((rows, FF_DIM), BF16)],
        compiler_params=_params(),
        name="ffn_ple",
    )(h1, p, st_ff, *consts, g_final)


def _prep_weights(g_mix, w_in, sc_conv_w, w_sc_out, ssm_conv_w, ssm_conv_b, ssm_dt_bias, ssm_a_log, ssm_d,
                  ssm_norm_g, w_ssm_out, cf_conv_w, cf_conv_b, cf_ln_g, cf_ln_b, w_cf_out, w_o, g_ffn, w_up,
                  ff_conv_w, ff_conv_b, w_down, g_ple, w_ple_gate, w_ple_proj):
    depth = w_in.shape[0]
    row = lambda v: v.reshape(depth, 1, -1)
    lane_pad = lambda v: jnp.pad(v, [(0, 0)] * (v.ndim - 1) + [(0, LANES - v.shape[-1])])
    head_of_channel = jnp.arange(SSM_INNER) // SSM_HEAD_DIM
    term_lane = jnp.arange(LANES)
    return {
        'g_mix': row(g_mix),
        'w_ag': w_in[:, :, 0:OFF_Z].astype(BF16).reshape(depth, D_MODEL, OFF_Z // SLAB, SLAB).transpose(0, 2, 1, 3),
        'w_z': w_in[:, :, OFF_Z:OFF_XBC].astype(BF16),
        'w_xbc': w_in[:, :, OFF_XBC:OFF_DT].astype(BF16),
        'w_dt': lane_pad(w_in[:, :, OFF_DT:OFF_CF]).astype(BF16),
        'w_cf': w_in[:, :, OFF_CF:N_IN].astype(BF16),
        'sc_conv_w': sc_conv_w,
        'w_sc_out': w_sc_out.astype(BF16),
        'ssm_conv_w': ssm_conv_w,
        'ssm_conv_b': row(ssm_conv_b),
        'dt_bias': lane_pad(row(ssm_dt_bias)),
        'a_log': lane_pad(row(ssm_a_log)),
        'd_exp': jnp.repeat(ssm_d, SSM_HEAD_DIM, axis=-1).reshape(depth, 1, SSM_INNER),
        'ssm_norm_g': row(ssm_norm_g),
        'w_ssm_out': w_ssm_out.astype(BF16),
        'cf_conv_w': cf_conv_w,
        'cf_conv_b': row(cf_conv_b),
        'cf_ln_g': row(cf_ln_g),
        'cf_ln_b': row(cf_ln_b),
        'w_cf_out': w_cf_out.astype(BF16),
        'w_o': w_o.astype(BF16),
        'g_ffn': row(g_ffn),
        'w_up': w_up.astype(BF16),
        'ff_conv_w': ff_conv_w,
        'ff_conv_b': row(ff_conv_b),
        'w_down': w_down.astype(BF16),
        'g_ple': row(g_ple),
        'w_ple_gate': w_ple_gate.astype(BF16),
        'w_ple_proj': w_ple_proj.astype(BF16),
        'head_expand': ((term_lane[:, None] % SSM_HEADS == head_of_channel[None, :])
                        & (term_lane[:, None] < SPLIT_TERMS * SSM_HEADS)).astype(BF16),
    }


def _run_trunk(x, p, st_sc, st_mc, st_ssm, st_cf, st_ff, w, g_final, tiles):
    depth, bsz = st_ssm.shape[:2]
    st_ssm = st_ssm.reshape(depth, bsz, SSM_INNER, SSM_STATE)
    h = x
    new_ssm = jnp.zeros(st_ssm.shape, F32)
    outs = [[], [], [], []]
    for i in range(depth):
        n, mac, gb, nsc, ncf = _ac_call(i, h, st_sc, st_cf, w, *tiles['ac'])
        h1, nmc, new_ssm = _ssm_call(i, n, h, mac, gb, st_mc, st_ssm, new_ssm, w, *tiles['ssm'])
        h, nff = _ffn_call(i, h1, p, st_ff, w, g_final, *tiles['ffn'], final=(i == depth - 1))
        for lst, s in zip(outs, (nsc, nmc, ncf, nff)):
            lst.append(s)
    nsc, nmc, ncf, nff = [jnp.stack(lst) for lst in outs]
    return h, [nsc, nmc, new_ssm.reshape(depth, bsz, SSM_HEADS, SSM_HEAD_DIM, SSM_STATE), ncf, nff]


def _tiles_for(bsz, tlen):
    if tlen >= 256:
        return {'ac': (1, 256), 'ssm': (1, 256), 'ffn': (1, 512 if tlen % 512 == 0 else 256)}
    return {'ac': (min(bsz, 16), tlen), 'ssm': (min(bsz, 4), tlen), 'ffn': (min(bsz, 16), tlen)}


def kernel(x_prompt, x_sample, p_prompt, p_sample, state_short_conv, state_ssm_conv, state_ssm, state_cf_conv, state_ffn_conv, g_mix, w_in, sc_conv_w, w_sc_out, ssm_conv_w, ssm_conv_b, ssm_dt_bias, ssm_a_log, ssm_d, ssm_norm_g, w_ssm_out, cf_conv_w, cf_conv_b, cf_ln_g, cf_ln_b, w_cf_out, w_o, g_ffn, w_up, ff_conv_w, ff_conv_b, w_down, g_ple, w_ple_gate, w_ple_proj, g_final):
    layers = _prep_weights(g_mix, w_in, sc_conv_w, w_sc_out, ssm_conv_w, ssm_conv_b, ssm_dt_bias, ssm_a_log,
                           ssm_d, ssm_norm_g, w_ssm_out, cf_conv_w, cf_conv_b, cf_ln_g, cf_ln_b, w_cf_out, w_o,
                           g_ffn, w_up, ff_conv_w, ff_conv_b, w_down, g_ple, w_ple_gate, w_ple_proj)
    gfin = g_final.reshape(1, D_MODEL)
    bp, tp, _ = x_prompt.shape
    bs, ts, _ = x_sample.shape
    z_sc = jnp.zeros((DEPTH, bp, SC_KERNEL - 1, D_MODEL), F32)
    z_mc = jnp.zeros((DEPTH, bp, SSM_CONV - 1, SSM_CONV_DIM), F32)
    z_ssm = jnp.zeros((DEPTH, bp, SSM_HEADS, SSM_HEAD_DIM, SSM_STATE), F32)
    z_cf = jnp.zeros((DEPTH, bp, CF_KERNEL - 1, D_MODEL), F32)
    z_ff = jnp.zeros((DEPTH, bp, FF_KERNEL - 1, 2 * FF_DIM), F32)
    y_p, sp = _run_trunk(x_prompt, p_prompt, z_sc, z_mc, z_ssm, z_cf, z_ff, layers, gfin, _tiles_for(bp, tp))
    y_s, ss = _run_trunk(x_sample, p_sample, state_short_conv, state_ssm_conv, state_ssm, state_cf_conv,
                         state_ffn_conv, layers, gfin, _tiles_for(bs, ts))
    return (y_p, y_s, sp[0], sp[1], sp[2], sp[3], sp[4], ss[0], ss[1], ss[2], ss[3], ss[4])
```

```python
import functools

import jax
import jax.numpy as jnp
from jax import lax
from jax.experimental import pallas as pl
from jax.experimental.pallas import tpu as pltpu

F32 = jnp.float32
BF16 = jnp.bfloat16

D_MODEL = 1024
DEPTH = 2
PLE_DIM = 256
EPS = 1e-6
SC_KERNEL = 3
SSM_INNER = 2 * D_MODEL
SSM_HEAD_DIM = 64
SSM_HEADS = SSM_INNER // SSM_HEAD_DIM
SSM_GROUPS = 4
SSM_STATE = 128
SSM_CONV = 4
SSM_GN = SSM_GROUPS * SSM_STATE
SSM_CONV_DIM = SSM_INNER + 2 * SSM_GN
CF_KERNEL = 31
FF_DIM = 2816
FF_KERNEL = 3
OFF_SC = 3 * D_MODEL
OFF_Z = OFF_SC + 3 * D_MODEL
OFF_XBC = OFF_Z + SSM_INNER
OFF_DT = OFF_XBC + SSM_CONV_DIM
OFF_CF = OFF_DT + SSM_HEADS
N_IN = OFF_CF + 2 * D_MODEL

LANES = 128
SUBLANES = 8
MXU_DIM = 256
SSD_CHUNK = 128
GROUP_W = SSM_INNER // SSM_GROUPS
SPLIT_TERMS = 3
SLAB = 2 * MXU_DIM
AC_STEPS = 4
VMEM_LIMIT = 56 * 1024 * 1024
LOG2_E = 1.4426950408889634


def _dot(a, b):
    return jnp.dot(a, b, preferred_element_type=F32)


def _sigmoid(x):
    return 0.5 * jnp.tanh(0.5 * x) + 0.5


def _silu(x):
    h = 0.5 * x
    return h + h * jnp.tanh(h)


def _softplus(x):
    return jnp.maximum(x, 0.0) + jnp.log1p(jnp.exp(-jnp.abs(x)))


def _rms(x, g):
    return x * lax.rsqrt(jnp.mean(x * x, axis=-1, keepdims=True) + EPS) * g


def _pad_rows(k):
    return -(-(k - 1) // SUBLANES) * SUBLANES


def _chunk_pos(i, cpb, rc):
    return i // cpb, (i % cpb) * rc


def _for_each(n, body):
    for i in range(n):
        body(i)


def _lane_blocks(c):
    return [slice(lb * LANES, (lb + 1) * LANES) for lb in range(c // LANES)]


def _load_conv_tail(xpad, pad, k, st_ref):
    for lb, ls in enumerate(_lane_blocks(st_ref.shape[-1])):
        xpad[:, lb, pad - (k - 1):pad, :] = st_ref[:, :, ls]


def _carry_conv_tail(xpad, pad, k, tt, new_ref, is_last):
    tail = xpad[:, :, pad + tt - (k - 1):pad + tt, :]

    @pl.when(is_last)
    def _():
        for lb, ls in enumerate(_lane_blocks(new_ref.shape[-1])):
            new_ref[:, :, ls] = tail[:, lb]

    xpad[:, :, pad - (k - 1):pad, :] = tail


def _store_tile(xpad, pad, val, nb, tt, first_block=0):
    for lb, ls in enumerate(_lane_blocks(val.shape[-1])):
        xpad[:, first_block + lb, pad:pad + tt, :] = val[:, ls].reshape(nb, tt, LANES)


def _ac_kernel(h_ref, stsc_ref, stcf_ref, gmix_ref, wag_ref, wcf_ref, scw_ref, wsco_ref,
               cfw_ref, cfb_ref, lng_ref, lnb_ref, wcfo_ref,
               n_ref, mac_ref, gb_ref, nsc_ref, ncf_ref,
               xsc, xcf, proj_ag, proj_c, conv_c, lhs_a, lhs_c, ya, yc, *, nb, tt, rc):
    t = pl.program_id(1)
    is_last = t == pl.num_programs(1) - 1
    rows = nb * tt
    cpb = tt // rc
    chunks = rows // rc
    psc, pcf = _pad_rows(SC_KERNEL), _pad_rows(CF_KERNEL)
    n_slabs = wag_ref.shape[0]
    steps = AC_STEPS
    cps, sps = chunks // steps, n_slabs // steps

    @pl.when(t == 0)
    def _():
        _load_conv_tail(xsc, psc, SC_KERNEL, stsc_ref)
        _load_conv_tail(xcf, pcf, CF_KERNEL, stcf_ref)

    h = h_ref[...].reshape(rows, D_MODEL)
    n_ref[...] = _rms(h, gmix_ref[...]).astype(BF16).reshape(nb, tt, D_MODEL)

    def n_tile():
        return n_ref[...].reshape(rows, D_MODEL)

    def ag(rs, col, width=LANES):
        return proj_ag[col // SLAB, rs, col % SLAB:col % SLAB + width]

    proj_c[...] = _dot(n_tile(), wcf_ref[...])

    def c_fill(i):
        b, r = _chunk_pos(i, cpb, rc)
        r2 = i * rc
        for lb, ls in enumerate(_lane_blocks(D_MODEL)):
            c_g = proj_c[pl.ds(r2, rc), D_MODEL + lb * LANES:D_MODEL + (lb + 1) * LANES]
            xcf[b, lb, pl.ds(pcf + r, rc), :] = proj_c[pl.ds(r2, rc), ls] * _sigmoid(c_g)

    _for_each(chunks, c_fill)

    def c_step(s, carry):
        for q in range(sps):
            proj_ag[s * sps + q] = _dot(n_tile(), wag_ref[s * sps + q])
        for lb, ls in enumerate(_lane_blocks(D_MODEL)):
            taps = [jnp.broadcast_to(cfw_ref[k:k + 1, ls], (SUBLANES, LANES)) for k in range(CF_KERNEL)]
            for q in range(cps):
                i = s * cps + q
                b, r = _chunk_pos(i, cpb, rc)
                for rg in range(0, rc, SUBLANES):
                    acc = None
                    for k in range(CF_KERNEL):
                        term = xcf[b, lb, pl.ds(pcf - (CF_KERNEL - 1) + k + r + rg, SUBLANES), :] * taps[k]
                        acc = term if acc is None else acc + term
                    conv_c[pl.ds(pl.multiple_of(i * rc + rg, SUBLANES), SUBLANES), ls] = acc
        for q in range(cps):
            r2 = pl.multiple_of((s * cps + q) * rc, rc)
            v = conv_c[pl.ds(r2, rc), :] + cfb_ref[...]
            mu = jnp.mean(v, axis=-1, keepdims=True)
            vc = v - mu
            var = jnp.mean(vc * vc, axis=-1, keepdims=True)
            ln = vc * lax.rsqrt(var + EPS) * lng_ref[...] + lnb_ref[...]
            lhs_c[pl.ds(r2, rc), :] = _silu(ln).astype(BF16)
        return carry

    lax.fori_loop(0, steps, c_step, 0)
    _carry_conv_tail(xcf, pcf, CF_KERNEL, tt, ncf_ref, is_last)

    def a_fill(i):
        b, r = _chunk_pos(i, cpb, rc)
        rs = pl.ds(i * rc, rc)
        for lb, ls in enumerate(_lane_blocks(D_MODEL)):
            xsc[b, lb, pl.ds(psc + r, rc), :] = (ag(rs, OFF_SC + D_MODEL + lb * LANES)
                                                 * ag(rs, OFF_SC + 2 * D_MODEL + lb * LANES))

    _for_each(chunks, a_fill)

    def a_conv(i):
        b, r = _chunk_pos(i, cpb, rc)
        rs = pl.ds(i * rc, rc)
        for lb, ls in enumerate(_lane_blocks(D_MODEL)):
            u = None
            for k in range(SC_KERNEL):
                term = xsc[b, lb, pl.ds(psc - (SC_KERNEL - 1) + k + r, rc), :] * scw_ref[k:k + 1, ls]
                u = term if u is None else u + term
            lhs_a[rs, ls] = (ag(rs, OFF_SC + lb * LANES) * u).astype(BF16)

    _for_each(chunks, a_conv)
    _carry_conv_tail(xsc, psc, SC_KERNEL, tt, nsc_ref, is_last)
    yc[...] = _dot(lhs_c[...], wcfo_ref[...])
    ya[...] = _dot(lhs_a[...], wsco_ref[...])

    def g_merge(i):
        b, r = _chunk_pos(i, cpb, rc)
        rs = pl.ds(i * rc, rc)
        for lo in range(0, D_MODEL, SLAB):
            cs = slice(lo, lo + SLAB)
            g_a = _sigmoid(ag(rs, lo, SLAB))
            g_b = _sigmoid(ag(rs, D_MODEL + lo, SLAB))
            g_c = _sigmoid(ag(rs, 2 * D_MODEL + lo, SLAB))
            mac_ref[b, pl.ds(r, rc), cs] = g_a * ya[rs, cs] + g_c * yc[rs, cs]
            gb_ref[b, pl.ds(r, rc), cs] = g_b

    _for_each(chunks, g_merge)


def _ssm_kernel(n_ref, h_ref, mac_ref, gb_ref, stmc_ref, stssm_ref,
                wz_ref, wxbc_ref, wdt_ref, cw_ref, cb_ref, dtb_ref, alog_ref, de_ref, ng_ref,
                wso_ref, wo_ref, e_ref, state_buf_ref,
                h1_ref, nmc_ref, nssm_ref,
                xpad, zs, xs_s, bs_s, cs_s, dt_s, acs_s, acst_s, dtt_s, ht, ys, *, nb, tt, rc):
    del state_buf_ref
    t = pl.program_id(1)
    is_last = t == pl.num_programs(1) - 1
    rows = nb * tt
    cpb = tt // rc
    pmc = _pad_rows(SSM_CONV)
    L = SSD_CHUNK
    seq_mode = tt >= L
    lv = L if seq_mode else tt
    n_chunks = rows // lv

    @pl.when(t == 0)
    def _():
        _load_conv_tail(xpad, pmc, SSM_CONV, stmc_ref)

    def load_state(b):
        for j in range(SSM_INNER // LANES):
            ht[:, j * LANES:(j + 1) * LANES] = stssm_ref[b, j * LANES:(j + 1) * LANES, :].T

    def store_state(b):
        for j in range(SSM_INNER // LANES):
            nssm_ref[b, j * LANES:(j + 1) * LANES, :] = ht[:, j * LANES:(j + 1) * LANES].T

    if seq_mode:
        @pl.when(t == 0)
        def _():
            load_state(0)
    else:
        xs_s[...] = jnp.zeros_like(xs_s)
        bs_s[...] = jnp.zeros_like(bs_s)
        cs_s[...] = jnp.zeros_like(cs_s)
        dt_s[...] = jnp.zeros_like(dt_s)

    def n_tile():
        return n_ref[...].reshape(rows, D_MODEL)

    _store_tile(xpad, pmc, _dot(n_tile(), wxbc_ref[...]), nb, tt)
    zs[...] = _dot(n_tile(), wz_ref[...])
    dt_all = _softplus(_dot(n_tile(), wdt_ref[...]) + dtb_ref[...])
    if seq_mode:
        dt_s[...] = dt_all
    else:
        for b in range(nb):
            dt_s[b * L:b * L + tt, :] = dt_all[b * tt:(b + 1) * tt, :]

    def conv(i):
        b, r = _chunk_pos(i, cpb, rc)
        dst = i * (rc if seq_mode else L)
        for lb, ls in enumerate(_lane_blocks(SSM_CONV_DIM)):
            acc = None
            for k in range(SSM_CONV):
                term = xpad[b, lb, pl.ds(pmc - (SSM_CONV - 1) + k + r, rc), :] * cw_ref[k:k + 1, ls]
                acc = term if acc is None else acc + term
            xbc = _silu(acc + cb_ref[:, ls])
            if ls.start < SSM_INNER:
                xs_s[pl.ds(dst, rc), ls] = xbc
            elif ls.start < SSM_INNER + SSM_GN:
                bs_s[pl.ds(dst, rc), ls.start - SSM_INNER:ls.stop - SSM_INNER] = xbc
            else:
                cs_s[pl.ds(dst, rc), ls.start - SSM_INNER - SSM_GN:ls.stop - SSM_INNER - SSM_GN] = xbc

    _for_each(rows // rc, conv)
    _carry_conv_tail(xpad, pmc, SSM_CONV, tt, nmc_ref, is_last)

    a_row = -jnp.exp(alog_ref[...])
    ii = lax.broadcasted_iota(jnp.int32, (L, L), 0)
    jj = lax.broadcasted_iota(jnp.int32, (L, L), 1)
    causal = ii >= jj
    tril = jnp.where(causal, 1.0, 0.0).astype(BF16)
    low_half = (lax.broadcasted_iota(jnp.int32, (1, GROUP_W), 1) % LANES) < SSM_HEAD_DIM
    even_cols = jnp.where(low_half, 1.0, 0.0).astype(BF16)
    odd_cols = jnp.where(low_half, 0.0, 1.0).astype(BF16)

    def pack_terms(x):
        head_lane = lax.broadcasted_iota(jnp.int32, x.shape, 1) < SSM_HEADS
        r = jnp.where(head_lane, x, 0.0)
        packed = None
        for k in range(SPLIT_TERMS):
            p = r.astype(BF16).astype(F32)
            placed = p if k == 0 else pltpu.roll(p, k * SSM_HEADS, 1)
            packed = placed if packed is None else packed + placed
            r = r - p
        return packed.astype(BF16)

    def expand(x):
        return _dot(pack_terms(x), e_ref[...])

    def chunk(c):
        if not seq_mode:
            load_state(c)
        r0 = c * L
        rv = c * lv
        dtc = dt_s[pl.ds(r0, L), :]
        cs3 = _dot(tril, pack_terms(dtc * a_row))
        acs = cs3
        for k in range(1, SPLIT_TERMS):
            acs = acs + pltpu.roll(cs3, LANES - k * SSM_HEADS, 1)
        acs2 = acs * LOG2_E
        acs_s[...] = acs2
        acst_s[...] = acs2.T
        dtt_s[...] = dtc.T
        a_last = acs[L - 1:L, :]
        eacs_e = expand(jnp.exp(acs))
        dtdec_e = expand(dtc * jnp.exp(a_last - acs))
        elast_e = eacs_e[L - 1:L, :]
        for g in range(SSM_GROUPS):
            gs = slice(g * GROUP_W, (g + 1) * GROUP_W)
            ns = slice(g * SSM_STATE, (g + 1) * SSM_STATE)
            xs_g = xs_s[pl.ds(r0, L), gs]
            xs_b = xs_g.astype(BF16)
            x_even = xs_b * even_cols
            x_odd = xs_b * odd_cols
            xdw_b = (xs_g * dtdec_e[:, gs]).astype(BF16)
            c_g = cs_s[pl.ds(r0, L), ns].astype(BF16)
            bt_g = bs_s[pl.ds(r0, L), ns].T.astype(BF16)
            cb = _dot(c_g, bt_g)
            ht_g = ht[:, gs]
            y_off = _dot(c_g, ht_g.astype(BF16)) * eacs_e[:, gs]
            ht[:, gs] = ht_g * elast_e[:, gs] + _dot(bt_g, xdw_b)
            y_cols = []
            for pr in range(GROUP_W // LANES):
                hd = g * (GROUP_W // SSM_HEAD_DIM) + 2 * pr
                ps = slice(pr * LANES, (pr + 1) * LANES)
                ms = []
                for hh in (hd, hd + 1):
                    diff = acs_s[:, hh:hh + 1] - acst_s[hh:hh + 1, :]
                    decay = jnp.exp2(jnp.where(causal, diff, -jnp.inf))
                    ms.append((cb * decay * dtt_s[hh:hh + 1, :]).astype(BF16))
                y_cols.append(_dot(jnp.concatenate(ms, axis=1),
                                   jnp.concatenate([x_even[:, ps], x_odd[:, ps]], axis=0)))
            y = jnp.concatenate(y_cols, axis=-1) + y_off + de_ref[:, gs] * xs_g
            y = y[0:lv, :] * _silu(zs[pl.ds(rv, lv), gs])
            y = y * lax.rsqrt(jnp.mean(y * y, axis=-1, keepdims=True) + EPS)
            ys[pl.ds(rv, lv), gs] = (y * ng_ref[:, gs]).astype(BF16)
        if not seq_mode:
            store_state(c)

    def finish(b, r, m):
        y_b = _dot(ys[pl.ds(b * tt + r, m), :], wso_ref[...])
        merged = mac_ref[b, pl.ds(r, m), :] + gb_ref[b, pl.ds(r, m), :] * y_b
        h1_ref[b, pl.ds(r, m), :] = h_ref[b, pl.ds(r, m), :] + _dot(merged.astype(BF16), wo_ref[...])

    for c in range(n_chunks):
        chunk(c)
        if seq_mode:
            finish(0, c * L, L)

    if seq_mode:
        @pl.when(is_last)
        def _():
            store_state(0)
    else:
        y_b = _dot(ys[...], wso_ref[...])
        merged = mac_ref[...].reshape(rows, D_MODEL) + gb_ref[...].reshape(rows, D_MODEL) * y_b
        h1 = h_ref[...].reshape(rows, D_MODEL) + _dot(merged.astype(BF16), wo_ref[...])
        h1_ref[...] = h1.reshape(nb, tt, D_MODEL)


def _ffn_kernel(h_ref, p_ref, stff_ref, gffn_ref, wup_ref, fw_ref, fb_ref, wdn_ref, gple_ref,
                wpg_ref, wpp_ref, gfin_ref,
                out_ref, nff_ref,
                xpad, nf_s, lhs, *, nb, tt, rc, final):
    t = pl.program_id(1)
    is_last = t == pl.num_programs(1) - 1
    rows = nb * tt
    cpb = tt // rc
    pff = _pad_rows(FF_KERNEL)
    half_blocks = FF_DIM // LANES
    slab_blocks = MXU_DIM // LANES
    n_slabs = FF_DIM // MXU_DIM

    @pl.when(t == 0)
    def _():
        _load_conv_tail(xpad, pff, FF_KERNEL, stff_ref)

    h1 = h_ref[...].reshape(rows, D_MODEL)
    nf_s[...] = _rms(h1, gffn_ref[...]).astype(BF16)

    def up_slab(j):
        for half in range(2):
            lo = half * FF_DIM + j * MXU_DIM
            _store_tile(xpad, pff, _dot(nf_s[...], wup_ref[:, lo:lo + MXU_DIM]), nb, tt, first_block=lo // LANES)

    def conv_slab(j):
        for i in range(rows // rc):
            b, r = _chunk_pos(i, cpb, rc)
            r2 = i * rc
            for lb in range(j * slab_blocks, (j + 1) * slab_blocks):
                halves = []
                for blk in (lb, half_blocks + lb):
                    cs = slice(blk * LANES, (blk + 1) * LANES)
                    acc = None
                    for k in range(FF_KERNEL):
                        term = xpad[b, blk, pl.ds(pff - (FF_KERNEL - 1) + k + r, rc), :] * fw_ref[k:k + 1, cs]
                        acc = term if acc is None else acc + term
                    halves.append(acc + fb_ref[:, cs])
                lhs[pl.ds(r2, rc), lb * LANES:(lb + 1) * LANES] = (_silu(halves[0]) * halves[1]).astype(BF16)

    up_slab(0)
    for j in range(n_slabs):
        if j + 1 < n_slabs:
            up_slab(j + 1)
        conv_slab(j)
    _carry_conv_tail(xpad, pff, FF_KERNEL, tt, nff_ref, is_last)

    h2 = h1 + _dot(lhs[...], wdn_ref[...])
    gate = _sigmoid(_dot(_rms(h2, gple_ref[...]).astype(BF16), wpg_ref[...]))
    pp = _dot(p_ref[...].reshape(rows, PLE_DIM).astype(BF16), wpp_ref[...])
    h3 = h2 + pp * gate
    if final:
        h3 = _rms(h3, gfin_ref[...])
    out_ref[...] = h3.reshape(nb, tt, D_MODEL)


def _tile_spec(nb, tt, c):
    return pl.BlockSpec((nb, tt, c), lambda b, t: (b, t, 0))


def _layer_tile_spec(layer, nb, tt, c):
    return pl.BlockSpec((None, nb, tt, c), lambda b, t: (layer, b, t, 0))


def _state_spec(nb, r, c):
    return pl.BlockSpec((nb, r, c), lambda b, t: (b, 0, 0))


def _layer_state_spec(layer, nb, r, c):
    return pl.BlockSpec((None, nb, r, c), lambda b, t: (layer, b, 0, 0))


def _const_spec(layer, arr):
    if layer is None:
        return pl.BlockSpec(arr.shape, lambda b, t: (0,) * arr.ndim, pipeline_mode=pl.Buffered(1))
    return pl.BlockSpec((None,) + arr.shape[1:], lambda b, t: (layer,) + (0,) * (arr.ndim - 1),
                        pipeline_mode=pl.Buffered(1))


def _params():
    return pltpu.CompilerParams(dimension_semantics=("arbitrary", "arbitrary"),
                                vmem_limit_bytes=VMEM_LIMIT)


def _conv_buffer(nb, k, tt, c):
    return pltpu.VMEM((nb, c // LANES, _pad_rows(k) + tt, LANES), F32)


def _ac_call(layer, h, st_sc, st_cf, w, nb, tt):
    bsz, tlen, _ = h.shape
    rc = min(tt, 32)
    rows = nb * tt
    consts = [w['g_mix'], w['w_ag'], w['w_cf'], w['sc_conv_w'], w['w_sc_out'],
              w['cf_conv_w'], w['cf_conv_b'], w['cf_ln_g'], w['cf_ln_b'], w['w_cf_out']]
    return pl.pallas_call(
        functools.partial(_ac_kernel, nb=nb, tt=tt, rc=rc),
        grid=(bsz // nb, tlen // tt),
        in_specs=[_tile_spec(nb, tt, D_MODEL), _layer_state_spec(layer, nb, SC_KERNEL - 1, D_MODEL),
                  _layer_state_spec(layer, nb, CF_KERNEL - 1, D_MODEL)] + [_const_spec(layer, a) for a in consts],
        out_specs=[_tile_spec(nb, tt, D_MODEL), _tile_spec(nb, tt, D_MODEL), _tile_spec(nb, tt, D_MODEL),
                   _state_spec(nb, SC_KERNEL - 1, D_MODEL), _state_spec(nb, CF_KERNEL - 1, D_MODEL)],
        out_shape=[jax.ShapeDtypeStruct((bsz, tlen, D_MODEL), BF16),
                   jax.ShapeDtypeStruct((bsz, tlen, D_MODEL), F32),
                   jax.ShapeDtypeStruct((bsz, tlen, D_MODEL), F32),
                   jax.ShapeDtypeStruct((bsz, SC_KERNEL - 1, D_MODEL), F32),
                   jax.ShapeDtypeStruct((bsz, CF_KERNEL - 1, D_MODEL), F32)],
        scratch_shapes=[_conv_buffer(nb, SC_KERNEL, tt, D_MODEL),
                        _conv_buffer(nb, CF_KERNEL, tt, D_MODEL),
                        pltpu.VMEM((OFF_Z // SLAB, rows, SLAB), F32),
                        pltpu.VMEM((rows, 2 * D_MODEL), F32),
                        pltpu.VMEM((rows, D_MODEL), F32),
                        pltpu.VMEM((rows, D_MODEL), BF16),
                        pltpu.VMEM((rows, D_MODEL), BF16),
                        pltpu.VMEM((rows, D_MODEL), F32),
                        pltpu.VMEM((rows, D_MODEL), F32)],
        compiler_params=_params(),
        name="mix_ac",
    )(h, st_sc, st_cf, *consts)


def _ssm_call(layer, n, h, mac, gb, st_mc, st_ssm, new_ssm, w, nb, tt):
    bsz, tlen, _ = h.shape
    rc = min(tt, 32)
    rows = nb * tt
    L = SSD_CHUNK
    prow = rows if tt >= L else nb * L
    consts = [w['w_z'], w['w_xbc'], w['w_dt'], w['ssm_conv_w'], w['ssm_conv_b'], w['dt_bias'], w['a_log'],
              w['d_exp'], w['ssm_norm_g'], w['w_ssm_out'], w['w_o']]
    inputs = [n, h, mac, gb, st_mc, st_ssm, *consts, w['head_expand'], new_ssm]
    in_specs = ([_tile_spec(nb, tt, D_MODEL)] * 4
                + [_layer_state_spec(layer, nb, SSM_CONV - 1, SSM_CONV_DIM),
                   _layer_state_spec(layer, nb, SSM_INNER, SSM_STATE)]
                + [_const_spec(layer, a) for a in consts] + [_const_spec(None, w['head_expand'])]
                + [pl.BlockSpec(memory_space=pl.ANY)])
    aliases = {len(inputs) - 1: 2}
    return pl.pallas_call(
        functools.partial(_ssm_kernel, nb=nb, tt=tt, rc=rc),
        grid=(bsz // nb, tlen // tt),
        in_specs=in_specs,
        out_specs=[_tile_spec(nb, tt, D_MODEL), _state_spec(nb, SSM_CONV - 1, SSM_CONV_DIM),
                   _layer_state_spec(layer, nb, SSM_INNER, SSM_STATE)],
        out_shape=[jax.ShapeDtypeStruct((bsz, tlen, D_MODEL), F32),
                   jax.ShapeDtypeStruct((bsz, SSM_CONV - 1, SSM_CONV_DIM), F32),
                   jax.ShapeDtypeStruct(st_ssm.shape, F32)],
        input_output_aliases=aliases,
        scratch_shapes=[_conv_buffer(nb, SSM_CONV, tt, SSM_CONV_DIM),
                        pltpu.VMEM((rows, SSM_INNER), F32),
                        pltpu.VMEM((prow, SSM_INNER), F32),
                        pltpu.VMEM((prow, SSM_GN), F32),
                        pltpu.VMEM((prow, SSM_GN), F32),
                        pltpu.VMEM((prow, LANES), F32),
                        pltpu.VMEM((L, LANES), F32),
                        pltpu.VMEM((LANES, L), F32),
                        pltpu.VMEM((LANES, L), F32),
                        pltpu.VMEM((SSM_STATE, SSM_INNER), F32),
                        pltpu.VMEM((rows, SSM_INNER), BF16)],
        compiler_params=_params(),
        name="mix_ssm",
    )(*inputs)


def _ffn_call(layer, h1, p, st_ff, w, g_final, nb, tt, final):
    bsz, tlen, _ = h1.shape
    rc = min(tt, 32)
    rows = nb * tt
    consts = [w['g_ffn'], w['w_up'], w['ff_conv_w'], w['ff_conv_b'], w['w_down'], w['g_ple'],
              w['w_ple_gate'], w['w_ple_proj']]
    return pl.pallas_call(
        functools.partial(_ffn_kernel, nb=nb, tt=tt, rc=rc, final=final),
        grid=(bsz // nb, tlen // tt),
        in_specs=[_tile_spec(nb, tt, D_MODEL), _layer_tile_spec(layer, nb, tt, PLE_DIM),
                  _layer_state_spec(layer, nb, FF_KERNEL - 1, 2 * FF_DIM)]
        + [_const_spec(layer, a) for a in consts] + [_const_spec(None, g_final)],
        out_specs=[_tile_spec(nb, tt, D_MODEL), _state_spec(nb, FF_KERNEL - 1, 2 * FF_DIM)],
        out_shape=[jax.ShapeDtypeStruct((bsz, tlen, D_MODEL), F32),
                   jax.ShapeDtypeStruct((bsz, FF_KERNEL - 1, 2 * FF_DIM), F32)],
        scratch_shapes=[_conv_buffer(nb, FF_KERNEL, tt, 2 * FF_DIM),
                        pltpu.VMEM((rows, D_MODEL), BF16),
                        pltpu.VMEM((rows, FF_DIM), BF16)],
        compiler_params=_params(),
        name="ffn_ple",
    )(h1, p, st_ff, *consts, g_final)


def _prep_weights(g_mix, w_in, sc_conv_w, w_sc_out, ssm_conv_w, ssm_conv_b, ssm_dt_bias, ssm_a_log, ssm_d,
                  ssm_norm_g, w_ssm_out, cf_conv_w, cf_conv_b, cf_ln_g, cf_ln_b, w_cf_out, w_o, g_ffn, w_up,
                  ff_conv_w, ff_conv_b, w_down, g_ple, w_ple_gate, w_ple_proj):
    depth = w_in.shape[0]
    row = lambda v: v.reshape(depth, 1, -1)
    lane_pad = lambda v: jnp.pad(v, [(0, 0)] * (v.ndim - 1) + [(0, LANES - v.shape[-1])])
    head_of_channel = jnp.arange(SSM_INNER) // SSM_HEAD_DIM
    term_lane = jnp.arange(LANES)
    return {
        'g_mix': row(g_mix),
        'w_ag': w_in[:, :, 0:OFF_Z].astype(BF16).reshape(depth, D_MODEL, OFF_Z // SLAB, SLAB).transpose(0, 2, 1, 3),
        'w_z': w_in[:, :, OFF_Z:OFF_XBC].astype(BF16),
        'w_xbc': w_in[:, :, OFF_XBC:OFF_DT].astype(BF16),
        'w_dt': lane_pad(w_in[:, :, OFF_DT:OFF_CF]).astype(BF16),
        'w_cf': w_in[:, :, OFF_CF:N_IN].astype(BF16),
        'sc_conv_w': sc_conv_w,
        'w_sc_out': w_sc_out.astype(BF16),
        'ssm_conv_w': ssm_conv_w,
        'ssm_conv_b': row(ssm_conv_b),
        'dt_bias': lane_pad(row(ssm_dt_bias)),
        'a_log': lane_pad(row(ssm_a_log)),
        'd_exp': jnp.repeat(ssm_d, SSM_HEAD_DIM, axis=-1).reshape(depth, 1, SSM_INNER),
        'ssm_norm_g': row(ssm_norm_g),
        'w_ssm_out': w_ssm_out.astype(BF16),
        'cf_conv_w': cf_conv_w,
        'cf_conv_b': row(cf_conv_b),
        'cf_ln_g': row(cf_ln_g),
        'cf_ln_b': row(cf_ln_b),
        'w_cf_out': w_cf_out.astype(BF16),
        'w_o': w_o.astype(BF16),
        'g_ffn': row(g_ffn),
        'w_up': w_up.astype(BF16),
        'ff_conv_w': ff_conv_w,
        'ff_conv_b': row(ff_conv_b),
        'w_down': w_down.astype(BF16),
        'g_ple': row(g_ple),
        'w_ple_gate': w_ple_gate.astype(BF16),
        'w_ple_proj': w_ple_proj.astype(BF16),
        'head_expand': ((term_lane[:, None] % SSM_HEADS == head_of_channel[None, :])
                        & (term_lane[:, None] < SPLIT_TERMS * SSM_HEADS)).astype(BF16),
    }


def _run_trunk(x, p, st_sc, st_mc, st_ssm, st_cf, st_ff, w, g_final, tiles):
    depth, bsz = st_ssm.shape[:2]
    st_ssm = st_ssm.reshape(depth, bsz, SSM_INNER, SSM_STATE)
    h = x
    new_ssm = jnp.zeros(st_ssm.shape, F32)
    outs = [[], [], [], []]
    for i in range(depth):
        n, mac, gb, nsc, ncf = _ac_call(i, h, st_sc, st_cf, w, *tiles['ac'])
        h1, nmc, new_ssm = _ssm_call(i, n, h, mac, gb, st_mc, st_ssm, new_ssm, w, *tiles['ssm'])
        h, nff = _ffn_call(i, h1, p, st_ff, w, g_final, *tiles['ffn'], final=(i == depth - 1))
        for lst, s in zip(outs, (nsc, nmc, ncf, nff)):
            lst.append(s)
    nsc, nmc, ncf, nff = [jnp.stack(lst) for lst in outs]
    return h, [nsc, nmc, new_ssm.reshape(depth, bsz, SSM_HEADS, SSM_HEAD_DIM, SSM_STATE), ncf, nff]


def _tiles_for(bsz, tlen):
    if tlen >= 256:
        return {'ac': (1, 256), 'ssm': (1, 256), 'ffn': (1, 512 if tlen % 512 == 0 else 256)}
    return {'ac': (min(bsz, 16), tlen), 'ssm': (min(bsz, 4), tlen), 'ffn': (min(bsz, 16), tlen)}


def kernel(x_prompt, x_sample, p_prompt, p_sample, state_short_conv, state_ssm_conv, state_ssm, state_cf_conv, state_ffn_conv, g_mix, w_in, sc_conv_w, w_sc_out, ssm_conv_w, ssm_conv_b, ssm_dt_bias, ssm_a_log, ssm_d, ssm_norm_g, w_ssm_out, cf_conv_w, cf_conv_b, cf_ln_g, cf_ln_b, w_cf_out, w_o, g_ffn, w_up, ff_conv_w, ff_conv_b, w_down, g_ple, w_ple_gate, w_ple_proj, g_final):
    layers = _prep_weights(g_mix, w_in, sc_conv_w, w_sc_out, ssm_conv_w, ssm_conv_b, ssm_dt_bias, ssm_a_log,
                           ssm_d, ssm_norm_g, w_ssm_out, cf_conv_w, cf_conv_b, cf_ln_g, cf_ln_b, w_cf_out, w_o,
                           g_ffn, w_up, ff_conv_w, ff_conv_b, w_down, g_ple, w_ple_gate, w_ple_proj)
    gfin = g_final.reshape(1, D_MODEL)
    bp, tp, _ = x_prompt.shape
    bs, ts, _ = x_sample.shape
    z_sc = jnp.zeros((DEPTH, bp, SC_KERNEL - 1, D_MODEL), F32)
    z_mc = jnp.zeros((DEPTH, bp, SSM_CONV - 1, SSM_CONV_DIM), F32)
    z_ssm = jnp.zeros((DEPTH, bp, SSM_HEADS, SSM_HEAD_DIM, SSM_STATE), F32)
    z_cf = jnp.zeros((DEPTH, bp, CF_KERNEL - 1, D_MODEL), F32)
    z_ff = jnp.zeros((DEPTH, bp, FF_KERNEL - 1, 2 * FF_DIM), F32)
    y_p, sp = _run_trunk(x_prompt, p_prompt, z_sc, z_mc, z_ssm, z_cf, z_ff, layers, gfin, _tiles_for(bp, tp))
    y_s, ss = _run_trunk(x_sample, p_sample, state_short_conv, state_ssm_conv, state_ssm, state_cf_conv,
                         state_ffn_conv, layers, gfin, _tiles_for(bs, ts))
    return (y_p, y_s, sp[0], sp[1], sp[2], sp[3], sp[4], ss[0], ss[1], ss[2], ss[3], ss[4])
```
